```python
import jax, jax.numpy as jnp
from jax import lax
import numpy as np

D_MODEL = 1024
BATCH = 8
SEQ = 2048
DEPTH = 4

N_META = 16
N_A_LAYERS = DEPTH // 2
N_B_LAYERS = DEPTH - N_A_LAYERS
LRU_WIDTH = D_MODEL
LRU_BLOCKS = 4
LRU_BLOCK = LRU_WIDTH // LRU_BLOCKS
CONV_WIDTH = 4
LRU_C = 8.0
ATTN_HEADS = 8
HEAD_DIM = 128
ATTN_WIDTH = ATTN_HEADS * HEAD_DIM
Q_BLOCK = 128
EPS = 1e-6

kernel_name = "yoco_hawk_fox_hybrid"


def rms_norm(x, g):
    xf = x.astype(jnp.float32)
    return xf * lax.rsqrt(jnp.mean(xf * xf, axis=-1, keepdims=True) + EPS) * g.astype(jnp.float32)


def causal_depthwise_conv(u, w, b):
    T = u.shape[1]
    up = jnp.pad(u, ((0, 0), (CONV_WIDTH - 1, 0), (0, 0)))
    out = b.astype(jnp.float32)
    for k in range(CONV_WIDTH):
        out = out + up[:, k:k + T] * w[k]
    return out


def block_diag_linear(u, w, b):
    B, T, _ = u.shape
    ub = u.reshape(B, T, LRU_BLOCKS, LRU_BLOCK)
    return jnp.einsum('btni,nij->btnj', ub, w).reshape(B, T, LRU_WIDTH) + b


def rg_lru(u, w_r, b_r, w_i, b_i, lam):
    r = jax.nn.sigmoid(block_diag_linear(u, w_r, b_r))
    i = jax.nn.sigmoid(block_diag_linear(u, w_i, b_i))
    log_a = -LRU_C * r * jax.nn.softplus(-lam.astype(jnp.float32))
    a = jnp.exp(log_a)
    mult = jnp.sqrt(-jnp.expm1(2.0 * log_a))
    bterm = mult * i * u

    def combine(left, right):
        a1, b1 = left
        a2, b2 = right
        return a1 * a2, a2 * b1 + b2

    _, h = lax.associative_scan(combine, (a, bterm), axis=1)
    return h


def recurrent_layer(x, g, w_in, conv_w, conv_b, w_r, b_r, w_i, b_i, lam, w_out):
    h = rms_norm(x, g)
    ug = h @ w_in
    u, gate = ug[..., :LRU_WIDTH], ug[..., LRU_WIDTH:]
    u = causal_depthwise_conv(u, conv_w, conv_b)
    y = rg_lru(u, w_r, b_r, w_i, b_i, lam)
    return (y * jax.nn.silu(gate)) @ w_out


def shared_kv(x, g_kv, w_kv, b_f, g_k):
    B, T, _ = x.shape
    h = rms_norm(x, g_kv)
    kvf = h @ w_kv
    k = kvf[..., :ATTN_WIDTH].reshape(B, T, ATTN_HEADS, HEAD_DIM)
    v = kvf[..., ATTN_WIDTH:2 * ATTN_WIDTH].reshape(B, T, ATTN_HEADS, HEAD_DIM)
    f_logit = kvf[..., 2 * ATTN_WIDTH:] + b_f
    k = rms_norm(k, g_k)
    cum = jnp.cumsum(jax.nn.log_sigmoid(f_logit.astype(jnp.float32)), axis=1)
    return (k.transpose(0, 2, 1, 3), v.astype(jnp.float32).transpose(0, 2, 1, 3), cum.transpose(0, 2, 1))


def forgetting_attention(q, k, v, cum):
    T = q.shape[2]
    scale = HEAD_DIM ** -0.5
    neg = jnp.finfo(jnp.float32).min
    bounds = [(0, N_META)] + [(s, min(s + Q_BLOCK, T)) for s in range(N_META, T, Q_BLOCK)]
    outs = []
    for s0, s1 in bounds:
        qb = q[:, :, s0:s1]
        kb = k[:, :, :s1]
        vb = v[:, :, :s1]
        logits = (jnp.einsum('bhqd,bhkd->bhqk', qb, kb) * scale
                  + cum[:, :, s0:s1, None] - cum[:, :, None, :s1])
        mask = jnp.arange(s0, s1)[:, None] >= jnp.arange(s1)[None, :]
        p = jax.nn.softmax(jnp.where(mask, logits, neg), axis=-1)
        outs.append(jnp.einsum('bhqk,bhkd->bhqd', p, vb))
    return jnp.concatenate(outs, axis=2)


def attention_layer(x, g, w_in, g_q, w_out, k, v, cum):
    B, T, _ = x.shape
    h = rms_norm(x, g)
    qg = h @ w_in
    q, gate = qg[..., :ATTN_WIDTH], qg[..., ATTN_WIDTH:]
    q = rms_norm(q.reshape(B, T, ATTN_HEADS, HEAD_DIM), g_q).transpose(0, 2, 1, 3)
    o = forgetting_attention(q, k, v, cum)
    o = o.transpose(0, 2, 1, 3).reshape(B, T, ATTN_WIDTH)
    return (o * jax.nn.silu(gate)) @ w_out


def setup_inputs(seed: int = 0) -> dict:
    key = jax.random.key(seed)
    ks = jax.random.split(key, 24)
    nA, nB = N_A_LAYERS, N_B_LAYERS
    f32 = jnp.float32

    def nrm(k, shape, scale):
        return jax.random.normal(k, shape, f32) * scale

    a8 = jax.random.uniform(ks[10], (nA, LRU_WIDTH), f32, 0.9, 0.999)
    a = a8 ** (1.0 / LRU_C)
    lam = jnp.log(a) - jnp.log1p(-a)
    b_f = jnp.linspace(1.0, 5.0, ATTN_HEADS, dtype=f32) + nrm(ks[14], (ATTN_HEADS,), 0.1)
    return {
        "x": nrm(ks[0], (BATCH, SEQ, D_MODEL), 1.0),
        "meta_tokens": nrm(ks[1], (N_META, D_MODEL), 1.0),
        "a_norm": 1.0 + nrm(ks[2], (nA, D_MODEL), 0.02),
        "a_w_in": nrm(ks[3], (nA, D_MODEL, 2 * LRU_WIDTH), D_MODEL ** -0.5),
        "a_conv_w": nrm(ks[4], (nA, CONV_WIDTH, LRU_WIDTH), CONV_WIDTH ** -0.5),
        "a_conv_b": nrm(ks[5], (nA, LRU_WIDTH), 0.01),
        "a_w_r": nrm(ks[6], (nA, LRU_BLOCKS, LRU_BLOCK, LRU_BLOCK), LRU_BLOCK ** -0.5),
        "a_b_r": nrm(ks[7], (nA, LRU_WIDTH), 0.01),
        "a_w_i": nrm(ks[8], (nA, LRU_BLOCKS, LRU_BLOCK, LRU_BLOCK), LRU_BLOCK ** -0.5),
        "a_b_i": nrm(ks[9], (nA, LRU_WIDTH), 0.01),
        "a_lambda": lam,
        "a_w_out": nrm(ks[11], (nA, LRU_WIDTH, D_MODEL), LRU_WIDTH ** -0.5),
        "kv_norm": 1.0 + nrm(ks[12], (D_MODEL,), 0.02),
        "w_kv": nrm(ks[13], (D_MODEL, 2 * ATTN_WIDTH + ATTN_HEADS), D_MODEL ** -0.5),
        "b_f": b_f,
        "k_norm": 1.0 + nrm(ks[15], (HEAD_DIM,), 0.02),
        "b_norm": 1.0 + nrm(ks[16], (nB, D_MODEL), 0.02),
        "b_w_in": nrm(ks[17], (nB, D_MODEL, 2 * ATTN_WIDTH), D_MODEL ** -0.5),
        "q_norm": 1.0 + nrm(ks[18], (nB, HEAD_DIM), 0.02),
        "b_w_out": nrm(ks[19], (nB, ATTN_WIDTH, D_MODEL), ATTN_WIDTH ** -0.5),
    }


def reference(x, meta_tokens, a_norm, a_w_in, a_conv_w, a_conv_b, a_w_r, a_b_r, a_w_i, a_b_i,
              a_lambda, a_w_out, kv_norm, w_kv, b_f, k_norm, b_norm, b_w_in, q_norm, b_w_out):
    B = x.shape[0]
    meta = jnp.broadcast_to(meta_tokens[None].astype(x.dtype), (B, N_META, D_MODEL))
    h = jnp.concatenate([meta, x], axis=1)
    k = v = cum = None
    for l in range(DEPTH):
        if l < N_A_LAYERS:
            out = recurrent_layer(h, a_norm[l], a_w_in[l], a_conv_w[l], a_conv_b[l], a_w_r[l],
                                  a_b_r[l], a_w_i[l], a_b_i[l], a_lambda[l], a_w_out[l])
        else:
            if l == N_A_LAYERS:
                k, v, cum = shared_kv(h, kv_norm, w_kv, b_f, k_norm)
            j = l - N_A_LAYERS
            out = attention_layer(h, b_norm[j], b_w_in[j], q_norm[j], b_w_out[j], k, v, cum)
        h = h + out.astype(h.dtype)
    return h[:, N_META:]
```

```python
import functools

import jax
import jax.numpy as jnp
from jax import lax
from jax.experimental import pallas as pl
from jax.experimental.pallas import tpu as pltpu

F32 = jnp.float32
BF16 = jnp.bfloat16

D_MODEL = 1024
LRU_WIDTH = 1024
LRU_BLOCKS = 4
LRU_BLOCK = LRU_WIDTH // LRU_BLOCKS
CONV_WIDTH = 4
LRU_C = 8.0
HEADS = 8
HEAD_DIM = 128
ATTN_WIDTH = HEADS * HEAD_DIM
N_META = 16
EPS = 1e-6

V7X_SUBLANES = 8
V7X_LANES = 128
AUG_LANES = V7X_LANES // HEADS
MASKED = -1e30

A_CHUNK = 512
KV_CHUNK = 512
ATT_BLOCK = 256
VMEM_LIMIT = 56 * 1024 * 1024


def _rms_scale(x):
    return lax.rsqrt(jnp.mean(x * x, axis=-1, keepdims=True) + EPS)


def _sigmoid(x):
    return 1.0 / (1.0 + jnp.exp(-x))


def _softplus(x):
    return jnp.maximum(x, 0.0) + jnp.log1p(jnp.exp(-jnp.abs(x)))


def _a_layer_kernel(x_ref, st0_ref, gn_ref, win_ref, cw_ref, cb_ref, wri_ref, bri_ref, lam_ref,
                    wout_ref, out_ref, *rest, chunk, emit_state):
    if emit_state:
        st_out_ref, ug_s, ubuf, a_s, b_s, hcar = rest
    else:
        ug_s, ubuf, a_s, b_s, hcar = rest
    c = pl.program_id(1)
    w = LRU_WIDTH

    @pl.when(c == 0)
    def _():
        hcar[...] = st0_ref[0:8, :]
        ubuf[0:8, :] = st0_ref[8:16, :]

    x = x_ref[0]
    xn = (x * _rms_scale(x) * gn_ref[...]).astype(BF16)
    ug_s[...] = jnp.dot(xn, win_ref[...], preferred_element_type=F32)

    u = ug_s[:, 0:w]
    ubuf[8:8 + chunk, :] = u
    uc = (cb_ref[...] + cw_ref[0:1, :] * ubuf[5:5 + chunk, :] + cw_ref[1:2, :] * ubuf[6:6 + chunk, :]
          + cw_ref[2:3, :] * ubuf[7:7 + chunk, :] + cw_ref[3:4, :] * u)
    ubuf[0:8, :] = ubuf[chunk:chunk + 8, :]

    ucb = uc.astype(BF16)
    neg_c_sp = -LRU_C * _softplus(-lam_ref[...])
    for n in range(LRU_BLOCKS):
        lo, hi = n * LRU_BLOCK, (n + 1) * LRU_BLOCK
        ri = jnp.dot(ucb[:, lo:hi], wri_ref[n], preferred_element_type=F32) + bri_ref[n]
        r = _sigmoid(ri[:, 0:LRU_BLOCK])
        i = _sigmoid(ri[:, LRU_BLOCK:2 * LRU_BLOCK])
        a = jnp.exp(r * neg_c_sp[:, lo:hi])
        a_s[:, lo:hi] = a
        b_s[:, lo:hi] = jnp.sqrt(1.0 - a * a) * i * uc[:, lo:hi]

    row = lax.broadcasted_iota(jnp.int32, (V7X_SUBLANES, w), 0)

    def tile_body(j, hprev):
        r0 = pl.multiple_of(j * V7X_SUBLANES, V7X_SUBLANES)
        a = a_s[pl.ds(r0, V7X_SUBLANES), :]
        b = b_s[pl.ds(r0, V7X_SUBLANES), :]
        for k in (1, 2, 4):
            keep = row >= k
            a_sh = jnp.where(keep, pltpu.roll(a, k, 0), 1.0)
            b_sh = jnp.where(keep, pltpu.roll(b, k, 0), 0.0)
            b = a * b_sh + b
            a = a * a_sh
        h = a * hprev + b
        b_s[pl.ds(r0, V7X_SUBLANES), :] = h
        return h[V7X_SUBLANES - 1:V7X_SUBLANES, :]

    hlast = lax.fori_loop(0, chunk // V7X_SUBLANES, tile_body, hcar[0:1, :], unroll=2)
    hcar[...] = jnp.broadcast_to(hlast, (V7X_SUBLANES, w))

    gate = ug_s[:, w:2 * w]
    y = (b_s[...] * (gate * _sigmoid(gate))).astype(BF16)
    out_ref[0] = x + jnp.dot(y, wout_ref[...], preferred_element_type=F32)

    if emit_state:
        @pl.when(c == pl.num_programs(1) - 1)
        def _():
            st_out_ref[0, 0:8, :] = hcar[...]
            st_out_ref[0, 8:16, :] = ubuf[0:8, :]


def _a_layer(x, st0, p, *, chunk, emit_state):
    bsz, t, d = x.shape
    assert t % chunk == 0 and chunk % V7X_SUBLANES == 0 and chunk >= 8
    w = LRU_WIDTH
    const2 = lambda b, c: (0, 0)
    const3 = lambda b, c: (0, 0, 0)
    in_specs = [
        pl.BlockSpec((1, chunk, d), lambda b, c: (b, c, 0)),
        pl.BlockSpec((16, w), const2),
        pl.BlockSpec((1, d), const2),
        pl.BlockSpec((d, 2 * w), const2),
        pl.BlockSpec((CONV_WIDTH, w), const2),
        pl.BlockSpec((1, w), const2),
        pl.BlockSpec((LRU_BLOCKS, LRU_BLOCK, 2 * LRU_BLOCK), const3),
        pl.BlockSpec((LRU_BLOCKS, 1, 2 * LRU_BLOCK), const3),
        pl.BlockSpec((1, w), const2),
        pl.BlockSpec((w, d), const2),
    ]
    out_shape = [jax.ShapeDtypeStruct((bsz, t, d), F32)]
    out_specs = [pl.BlockSpec((1, chunk, d), lambda b, c: (b, c, 0))]
    if emit_state:
        out_shape.append(jax.ShapeDtypeStruct((bsz, 16, w), F32))
        out_specs.append(pl.BlockSpec((1, 16, w), lambda b, c: (b, 0, 0)))
    scratch = [
        pltpu.VMEM((chunk, 2 * w), F32),
        pltpu.VMEM((chunk + 8, w), F32),
        pltpu.VMEM((chunk, w), F32),
        pltpu.VMEM((chunk, w), F32),
        pltpu.VMEM((V7X_SUBLANES, w), F32),
    ]
    res = pl.pallas_call(
        functools.partial(_a_layer_kernel, chunk=chunk, emit_state=emit_state),
        grid=(bsz, t // chunk),
        in_specs=in_specs,
        out_specs=out_specs,
        out_shape=out_shape,
        scratch_shapes=scratch,
        compiler_params=pltpu.CompilerParams(
            dimension_semantics=("arbitrary", "arbitrary"), vmem_limit_bytes=VMEM_LIMIT),
        name="rglru_layer_state" if emit_state else "rglru_layer",
    )(x, st0, p["gn"], p["win"], p["cw"], p["cb"], p["wri"], p["bri"], p["lam"], p["wout"])
    return res if emit_state else res[0]


def _kv_kernel(h_ref, c0_ref, gn_ref, wk_ref, wv_ref, wf_ref, bf_ref, gk_ref,
               kt_ref, kat_ref, v_ref, qa_ref, cfin_ref, ccar, *, chunk, kblock, n_valid):
    c = pl.program_id(1)

    @pl.when(c == 0)
    def _():
        ccar[...] = c0_ref[...]

    x = h_ref[0]
    xn = (x * _rms_scale(x) * gn_ref[...]).astype(BF16)

    k = jnp.dot(xn, wk_ref[...], preferred_element_type=F32)
    for hd in range(HEADS):
        kh = k[:, hd * HEAD_DIM:(hd + 1) * HEAD_DIM]
        kht = (kh * _rms_scale(kh) * gk_ref[...]).T
        for kb in range(chunk // kblock):
            kt_ref[0, hd, kb] = kht[:, kb * kblock:(kb + 1) * kblock].astype(BF16)

    v = jnp.dot(xn, wv_ref[...], preferred_element_type=F32).astype(BF16)
    for hd in range(HEADS):
        v_ref[0, hd] = v[:, hd * HEAD_DIM:(hd + 1) * HEAD_DIM]

    f = jnp.dot(xn, wf_ref[...], preferred_element_type=F32) + bf_ref[...]
    cum = jnp.minimum(f, 0.0) - jnp.log1p(jnp.exp(-jnp.abs(f)))
    row = lax.broadcasted_iota(jnp.int32, (chunk, V7X_LANES), 0)
    k_step = 1
    while k_step < chunk:
        cum = cum + jnp.where(row >= k_step, pltpu.roll(cum, k_step, 0), 0.0)
        k_step *= 2
    cum = cum + ccar[0:1, :]
    last = n_valid - 1 if n_valid < chunk else chunk - 1
    ccar[...] = jnp.broadcast_to(cum[last:last + 1, :], (V7X_SUBLANES, V7X_LANES))

    hi = cum.astype(BF16).astype(F32)
    mid = (cum - hi).astype(BF16).astype(F32)
    lo = (cum - hi - mid).astype(BF16).astype(F32)
    sub = lax.broadcasted_iota(jnp.int32, (chunk, V7X_LANES), 1) & (AUG_LANES - 1)
    one = jnp.ones_like(cum)
    zero = jnp.zeros_like(cum)
    qa = jnp.where(sub < 3, one, jnp.where(sub == 3, hi, jnp.where(sub == 4, mid,
                                                                  jnp.where(sub == 5, lo, zero))))
    ka = jnp.where(sub == 0, -hi, jnp.where(sub == 1, -mid, jnp.where(sub == 2, -lo,
                                                                      jnp.where(sub < 6, one, zero))))
    if n_valid < chunk:
        ka = jnp.where((row >= n_valid) & (sub == 0), MASKED, ka)
    qa_ref[0] = qa.astype(BF16)
    kat = ka.T
    for kb in range(chunk // kblock):
        kat_ref[0, kb] = kat[:, kb * kblock:(kb + 1) * kblock].astype(BF16)

    @pl.when(c == pl.num_programs(1) - 1)
    def _():
        cfin_ref[0] = ccar[...]


def _kv_proj(h, c0, p, *, chunk, kblock, n_valid):
    bsz, t, d = h.shape
    assert t % chunk == 0 and chunk % kblock == 0
    assert n_valid == t or t == chunk
    nkb = t // kblock
    const2 = lambda b, c: (0, 0)
    cpk = chunk // kblock
    outs = pl.pallas_call(
        functools.partial(_kv_kernel, chunk=chunk, kblock=kblock, n_valid=n_valid),
        grid=(bsz, t // chunk),
        in_specs=[
            pl.BlockSpec((1, chunk, d), lambda b, c: (b, c, 0)),
            pl.BlockSpec((V7X_SUBLANES, V7X_LANES), const2),
            pl.BlockSpec((1, d), const2),
            pl.BlockSpec((d, ATTN_WIDTH), const2),
            pl.BlockSpec((d, ATTN_WIDTH), const2),
            pl.BlockSpec((d, V7X_LANES), const2),
            pl.BlockSpec((1, V7X_LANES), const2),
            pl.BlockSpec((1, HEAD_DIM), const2),
        ],
        out_specs=[
            pl.BlockSpec((1, HEADS, cpk, HEAD_DIM, kblock), lambda b, c: (b, 0, c, 0, 0)),
            pl.BlockSpec((1, cpk, V7X_LANES, kblock), lambda b, c: (b, c, 0, 0)),
            pl.BlockSpec((1, HEADS, chunk, HEAD_DIM), lambda b, c: (b, 0, c, 0)),
            pl.BlockSpec((1, chunk, V7X_LANES), lambda b, c: (b, c, 0)),
            pl.BlockSpec((1, V7X_SUBLANES, V7X_LANES), lambda b, c: (b, 0, 0)),
        ],
        out_shape=[
            jax.ShapeDtypeStruct((bsz, HEADS, nkb, HEAD_DIM, kblock), BF16),
            jax.ShapeDtypeStruct((bsz, nkb, V7X_LANES, kblock), BF16),
            jax.ShapeDtypeStruct((bsz, HEADS, t, HEAD_DIM), BF16),
            jax.ShapeDtypeStruct((bsz, t, V7X_LANES), BF16),
            jax.ShapeDtypeStruct((bsz, V7X_SUBLANES, V7X_LANES), F32),
        ],
        scratch_shapes=[pltpu.VMEM((V7X_SUBLANES, V7X_LANES), F32)],
        compiler_params=pltpu.CompilerParams(
            dimension_semantics=("arbitrary", "arbitrary"), vmem_limit_bytes=VMEM_LIMIT),
        name="shared_kv" if n_valid == t else "shared_kv_meta",
    )(h, c0, p["gn"], p["wk"], p["wv"], p["wf"], p["bf"], p["gk"])
    return outs


def _b_layer_kernel(x_ref, qa_ref, kt_ref, kat_ref, v_ref, ktm_ref, katm_ref, vm_ref,
                    gn_ref, win_ref, gq_ref, wout_ref, out_ref,
                    qg_s, q_s, o_s, m_s, l_s, acc_s, *, blk):
    qi = pl.program_id(1)
    aw = ATTN_WIDTH
    scale = HEAD_DIM ** -0.5

    x = x_ref[0]
    xn = (x * _rms_scale(x) * gn_ref[...]).astype(BF16)
    qg_s[...] = jnp.dot(xn, win_ref[...], preferred_element_type=F32)

    qa = qa_ref[0]
    lane_head = lax.shift_right_logical(
        lax.broadcasted_iota(jnp.int32, (blk, V7X_LANES), 1), AUG_LANES.bit_length() - 1)
    for hd in range(HEADS):
        qh = qg_s[:, hd * HEAD_DIM:(hd + 1) * HEAD_DIM]
        qn = qh * _rms_scale(qh) * (gq_ref[...] * scale)
        q_s[hd, :, 0:HEAD_DIM] = qn.astype(BF16)
        q_s[hd, :, HEAD_DIM:2 * HEAD_DIM] = jnp.where(lane_head == hd, qa, jnp.zeros_like(qa))

    rows = lax.broadcasted_iota(jnp.int32, (blk, blk), 0)
    cols = lax.broadcasted_iota(jnp.int32, (blk, blk), 1)
    causal = cols <= rows

    def online_update(s, v_blk):
        m_prev = m_s[...]
        m_next = jnp.maximum(m_prev, jnp.max(s, axis=1, keepdims=True))
        alpha = jnp.exp(m_prev - m_next)
        p = jnp.concatenate(
            [jnp.exp(s[:, j * V7X_LANES:(j + 1) * V7X_LANES] - m_next)
             for j in range(s.shape[1] // V7X_LANES)], axis=1)
        l_s[...] = alpha * l_s[...] + jnp.sum(p, axis=1, keepdims=True)
        acc_s[...] = alpha * acc_s[...] + jnp.dot(p.astype(BF16), v_blk,
                                                  preferred_element_type=F32)
        m_s[...] = m_next

    def head_body(hd, carry):
        q = q_s[hd]
        ktm = jnp.concatenate([ktm_ref[0, hd, 0], katm_ref[0, 0]], axis=0)
        s = jnp.dot(q, ktm, preferred_element_type=F32)
        m0 = jnp.max(s, axis=1, keepdims=True)
        p = jnp.exp(s - m0)
        m_s[...] = jnp.broadcast_to(m0, (blk, V7X_LANES))
        l_s[...] = jnp.broadcast_to(jnp.sum(p, axis=1, keepdims=True), (blk, V7X_LANES))
        acc_s[...] = jnp.dot(p.astype(BF16), vm_ref[0, hd], preferred_element_type=F32)

        def kv_body(kj, c2):
            kt = jnp.concatenate([kt_ref[0, hd, kj], kat_ref[0, kj]], axis=0)
            s = jnp.dot(q, kt, preferred_element_type=F32)
            r0 = pl.multiple_of(kj * blk, blk)
            online_update(s, v_ref[0, hd, pl.ds(r0, blk), :])
            return c2

        lax.fori_loop(0, qi, kv_body, 0)

        kt = jnp.concatenate([kt_ref[0, hd, qi], kat_ref[0, qi]], axis=0)
        s = jnp.dot(q, kt, preferred_element_type=F32)
        s = jnp.where(causal, s, MASKED)
        r0 = pl.multiple_of(qi * blk, blk)
        online_update(s, v_ref[0, hd, pl.ds(r0, blk), :])

        o_s[hd] = acc_s[...] / l_s[...]
        return carry

    lax.fori_loop(0, HEADS, head_body, 0)

    ys = []
    for hd in range(HEADS):
        gate = qg_s[:, aw + hd * HEAD_DIM:aw + (hd + 1) * HEAD_DIM]
        ys.append((o_s[hd] * (gate * _sigmoid(gate))).astype(BF16))
    y = jnp.concatenate(ys, axis=1)
    out_ref[0] = x + jnp.dot(y, wout_ref[...], preferred_element_type=F32)


def _b_layer(x, qa, kt, kat, v, ktm, katm, vm, p, *, blk):
    bsz, t, d = x.shape
    assert t % blk == 0
    nkb = t // blk
    aw = ATTN_WIDTH
    const2 = lambda b, q: (0, 0)
    return pl.pallas_call(
        functools.partial(_b_layer_kernel, blk=blk),
        grid=(bsz, t // blk),
        in_specs=[
            pl.BlockSpec((1, blk, d), lambda b, q: (b, q, 0)),
            pl.BlockSpec((1, blk, V7X_LANES), lambda b, q: (b, q, 0)),
            pl.BlockSpec((1, HEADS, nkb, HEAD_DIM, blk), lambda b, q: (b, 0, 0, 0, 0)),
            pl.BlockSpec((1, nkb, V7X_LANES, blk), lambda b, q: (b, 0, 0, 0)),
            pl.BlockSpec((1, HEADS, t, HEAD_DIM), lambda b, q: (b, 0, 0, 0)),
            pl.BlockSpec((1, HEADS, 1, HEAD_DIM, V7X_LANES), lambda b, q: (0, 0, 0, 0, 0)),
            pl.BlockSpec((1, 1, V7X_LANES, V7X_LANES), lambda b, q: (0, 0, 0, 0)),
            pl.BlockSpec((1, HEADS, V7X_LANES, HEAD_DIM), lambda b, q: (0, 0, 0, 0)),
            pl.BlockSpec((1, d), const2),
            pl.BlockSpec((d, 2 * aw), const2),
            pl.BlockSpec((1, HEAD_DIM), const2),
            pl.BlockSpec((aw, d), const2),
        ],
        out_specs=pl.BlockSpec((1, blk, d), lambda b, q: (b, q, 0)),
        out_shape=jax.ShapeDtypeStruct((bsz, t, d), F32),
        scratch_shapes=[
            pltpu.VMEM((blk, 2 * aw), F32),
            pltpu.VMEM((HEADS, blk, 2 * HEAD_DIM), BF16),
            pltpu.VMEM((HEADS, blk, HEAD_DIM), F32),
            pltpu.VMEM((blk, V7X_LANES), F32),
            pltpu.VMEM((blk, V7X_LANES), F32),
            pltpu.VMEM((blk, HEAD_DIM), F32),
        ],
        compiler_params=pltpu.CompilerParams(
            dimension_semantics=("arbitrary", "arbitrary"), vmem_limit_bytes=VMEM_LIMIT),
        name="fox_attention_layer",
    )(x, qa, kt, kat, v, ktm, katm, vm, p["gn"], p["win"], p["gq"], p["wout"])


def _a_params(l, a_norm, a_w_in, a_conv_w, a_conv_b, a_w_r, a_b_r, a_w_i, a_b_i, a_lambda, a_w_out):
    wri = jnp.concatenate([a_w_r[l], a_w_i[l]], axis=-1).astype(BF16)
    bri = jnp.concatenate([a_b_r[l].reshape(LRU_BLOCKS, 1, LRU_BLOCK),
                           a_b_i[l].reshape(LRU_BLOCKS, 1, LRU_BLOCK)], axis=-1).astype(F32)
    return dict(gn=a_norm[l].reshape(1, -1).astype(F32), win=a_w_in[l].astype(BF16),
                cw=a_conv_w[l].astype(F32), cb=a_conv_b[l].reshape(1, -1).astype(F32),
                wri=wri, bri=bri, lam=a_lambda[l].reshape(1, -1).astype(F32),
                wout=a_w_out[l].astype(BF16))


def _kv_params(kv_norm, w_kv, b_f, k_norm):
    aw = ATTN_WIDTH
    pad = jnp.zeros((D_MODEL, AUG_LANES - 6), F32)
    wf = w_kv[:, 2 * aw:2 * aw + HEADS].astype(F32)
    wf = jnp.concatenate([jnp.concatenate([wf[:, h:h + 1]] * 6 + [pad], axis=1) for h in range(HEADS)],
                         axis=1)
    bfp = jnp.concatenate([jnp.concatenate([b_f[h:h + 1].astype(F32)] * 6 + [jnp.zeros((AUG_LANES - 6,), F32)])
                           for h in range(HEADS)]).reshape(1, V7X_LANES)
    return dict(gn=kv_norm.reshape(1, -1).astype(F32), wk=w_kv[:, 0:aw].astype(BF16),
                wv=w_kv[:, aw:2 * aw].astype(BF16), wf=wf.astype(BF16), bf=bfp,
                gk=k_norm.reshape(1, -1).astype(F32))


def kernel(x, meta_tokens, a_norm, a_w_in, a_conv_w, a_conv_b, a_w_r, a_b_r, a_w_i, a_b_i, a_lambda,
           a_w_out, kv_norm, w_kv, b_f, k_norm, b_norm, b_w_in, q_norm, b_w_out):
    n_a = a_norm.shape[0]
    n_b = b_norm.shape[0]
    h = x.astype(F32)
    hm = meta_tokens[None].astype(F32)
    zero_state = jnp.zeros((16, LRU_WIDTH), F32)

    for l in range(n_a):
        p = _a_params(l, a_norm, a_w_in, a_conv_w, a_conv_b, a_w_r, a_b_r, a_w_i, a_b_i, a_lambda,
                      a_w_out)
        hm, st = _a_layer(hm, zero_state, p, chunk=N_META, emit_state=True)
        h = _a_layer(h, st[0], p, chunk=A_CHUNK, emit_state=False)

    pkv = _kv_params(kv_norm, w_kv, b_f, k_norm)
    hm_pad = jnp.pad(hm, ((0, 0), (0, V7X_LANES - N_META), (0, 0)))
    c_zero = jnp.zeros((V7X_SUBLANES, V7X_LANES), F32)
    ktm, katm, vm, _, c_meta = _kv_proj(hm_pad, c_zero, pkv, chunk=V7X_LANES, kblock=V7X_LANES,
                                        n_valid=N_META)
    kt, kat, v, qa, _ = _kv_proj(h, c_meta[0], pkv, chunk=KV_CHUNK, kblock=ATT_BLOCK,
                                 n_valid=h.shape[1])

    for j in range(n_b):
        p = dict(gn=b_norm[j].reshape(1, -1).astype(F32), win=b_w_in[j].astype(BF16),
                 gq=q_norm[j].reshape(1, -1).astype(F32), wout=b_w_out[j].astype(BF16))
        h = _b_layer(h, qa, kt, kat, v, ktm, katm, vm, p, blk=ATT_BLOCK)
    return h
```

```python
import functools

import jax
import jax.numpy as jnp
from jax import lax
from jax.experimental import pallas as pl
from jax.experimental.pallas import tpu as pltpu

F32 = jnp.float32
BF16 = jnp.bfloat16

D_MODEL = 1024
LRU_WIDTH = 1024
LRU_BLOCKS = 4
LRU_BLOCK = LRU_WIDTH // LRU_BLOCKS
CONV_WIDTH = 4
LRU_C = 8.0
HEADS = 8
HEAD_DIM = 128
ATTN_WIDTH = HEADS * HEAD_DIM
N_META = 16
EPS = 1e-6

V7X_SUBLANES = 8
V7X_LANES = 128
AUG_LANES = V7X_LANES // HEADS
MASKED = -1e30

A_CHUNK = 512
KV_CHUNK = 512
ATT_BLOCK = 256
VMEM_LIMIT = 56 * 1024 * 1024


def _rms_scale(x):
    return lax.rsqrt(jnp.mean(x * x, axis=-1, keepdims=True) + EPS)


def _sigmoid(x):
    return 1.0 / (1.0 + jnp.exp(-x))


def _softplus(x):
    return jnp.maximum(x, 0.0) + jnp.log1p(jnp.exp(-jnp.abs(x)))


def _a_layer_kernel(x_ref, st0_ref, gn_ref, win_ref, cw_ref, cb_ref, wri_ref, bri_ref, lam_ref,
                    wout_ref, out_ref, *rest, chunk, emit_state):
    if emit_state:
        st_out_ref, ug_s, ubuf, a_s, b_s, hcar = rest
    else:
        ug_s, ubuf, a_s, b_s, hcar = rest
    c = pl.program_id(1)
    w = LRU_WIDTH

    @pl.when(c == 0)
    def _():
        hcar[...] = st0_ref[0:8, :]
        ubuf[0:8, :] = st0_ref[8:16, :]

    x = x_ref[0]
    xn = (x * _rms_scale(x) * gn_ref[...]).astype(BF16)
    ug_s[...] = jnp.dot(xn, win_ref[...], preferred_element_type=F32)

    u = ug_s[:, 0:w]
    ubuf[8:8 + chunk, :] = u
    uc = (cb_ref[...] + cw_ref[0:1, :] * ubuf[5:5 + chunk, :] + cw_ref[1:2, :] * ubuf[6:6 + chunk, :]
          + cw_ref[2:3, :] * ubuf[7:7 + chunk, :] + cw_ref[3:4, :] * u)
    ubuf[0:8, :] = ubuf[chunk:chunk + 8, :]

    ucb = uc.astype(BF16)
    neg_c_sp = -LRU_C * _softplus(-lam_ref[...])
    for n in range(LRU_BLOCKS):
        lo, hi = n * LRU_BLOCK, (n + 1) * LRU_BLOCK
        ri = jnp.dot(ucb[:, lo:hi], wri_ref[n], preferred_element_type=F32) + bri_ref[n]
        r = _sigmoid(ri[:, 0:LRU_BLOCK])
        i = _sigmoid(ri[:, LRU_BLOCK:2 * LRU_BLOCK])
        a = jnp.exp(r * neg_c_sp[:, lo:hi])
        a_s[:, lo:hi] = a
        b_s[:, lo:hi] = jnp.sqrt(1.0 - a * a) * i * uc[:, lo:hi]

    row = lax.broadcasted_iota(jnp.int32, (V7X_SUBLANES, w), 0)

    def tile_body(j, hprev):
        r0 = pl.multiple_of(j * V7X_SUBLANES, V7X_SUBLANES)
        a = a_s[pl.ds(r0, V7X_SUBLANES), :]
        b = b_s[pl.ds(r0, V7X_SUBLANES), :]
        for k in (1, 2, 4):
            keep = row >= k
            a_sh = jnp.where(keep, pltpu.roll(a, k, 0), 1.0)
            b_sh = jnp.where(keep, pltpu.roll(b, k, 0), 0.0)
            b = a * b_sh + b
            a = a * a_sh
        h = a * hprev + b
        b_s[pl.ds(r0, V7X_SUBLANES), :] = h
        return h[V7X_SUBLANES - 1:V7X_SUBLANES, :]

    hlast = lax.fori_loop(0, chunk // V7X_SUBLANES, tile_body, hcar[0:1, :], unroll=2)
    hcar[...] = jnp.broadcast_to(hlast, (V7X_SUBLANES, w))

    gate = ug_s[:, w:2 * w]
    y = (b_s[...] * (gate * _sigmoid(gate))).astype(BF16)
    out_ref[0] = x + jnp.dot(y, wout_ref[...], preferred_element_type=F32)

    if emit_state:
        @pl.when(c == pl.num_programs(1) - 1)
        def _():
            st_out_ref[0, 0:8, :] = hcar[...]
            st_out_ref[0, 8:16, :] = ubuf[0:8, :]


def _a_layer(x, st0, p, *, chunk, emit_state):
    bsz, t, d = x.shape
    assert t % chunk == 0 and chunk % V7X_SUBLANES == 0 and chunk >= 8
    w = LRU_WIDTH
    const2 = lambda b, c: (0, 0)
    const3 = lambda b, c: (0, 0, 0)
    in_specs = [
        pl.BlockSpec((1, chunk, d), lambda b, c: (b, c, 0)),
        pl.BlockSpec((16, w), const2),
        pl.BlockSpec((1, d), const2),
        pl.BlockSpec((d, 2 * w), const2),
        pl.BlockSpec((CONV_WIDTH, w), const2),
        pl.BlockSpec((1, w), const2),
        pl.BlockSpec((LRU_BLOCKS, LRU_BLOCK, 2 * LRU_BLOCK), const3),
        pl.BlockSpec((LRU_BLOCKS, 1, 2 * LRU_BLOCK), const3),
        pl.BlockSpec((1, w), const2),
        pl.BlockSpec((w, d), const2),
    ]
    out_shape = [jax.ShapeDtypeStruct((bsz, t, d), F32)]
    out_specs = [pl.BlockSpec((1, chunk, d), lambda b, c: (b, c, 0))]
    if emit_state:
        out_shape.append(jax.ShapeDtypeStruct((bsz, 16, w), F32))
        out_specs.append(pl.BlockSpec((1, 16, w), lambda b, c: (b, 0, 0)))
    scratch = [
        pltpu.VMEM((chunk, 2 * w), F32),
        pltpu.VMEM((chunk + 8, w), F32),
        pltpu.VMEM((chunk, w), F32),
        pltpu.VMEM((chunk, w), F32),
        pltpu.VMEM((V7X_SUBLANES, w), F32),
    ]
    res = pl.pallas_call(
        functools.partial(_a_layer_kernel, chunk=chunk, emit_state=emit_state),
        grid=(bsz, t // chunk),
        in_specs=in_specs,
        out_specs=out_specs,
        out_shape=out_shape,
        scratch_shapes=scratch,
        compiler_params=pltpu.CompilerParams(
            dimension_semantics=("arbitrary", "arbitrary"), vmem_limit_bytes=VMEM_LIMIT),
        name="rglru_layer_state" if emit_state else "rglru_layer",
    )(x, st0, p["gn"], p["win"], p["cw"], p["cb"], p["wri"], p["bri"], p["lam"], p["wout"])
    return res if emit_state else res[0]


def _kv_kernel(h_ref, c0_ref, gn_ref, wk_ref, wv_ref, wf_ref, bf_ref, gk_ref,
               kt_ref, kat_ref, v_ref, qa_ref, cfin_ref, ccar, *, chunk, kblock, n_valid):
    c = pl.program_id(1)

    @pl.when(c == 0)
    def _():
        ccar[...] = c0_ref[...]

    x = h_ref[0]
    xn = (x * _rms_scale(x) * gn_ref[...]).astype(BF16)

    k = jnp.dot(xn, wk_ref[...], preferred_element_type=F32)
    for hd in range(HEADS):
        kh = k[:, hd * HEAD_DIM:(hd + 1) * HEAD_DIM]
        kht = (kh * _rms_scale(kh) * gk_ref[...]).T
        for kb in range(chunk // kblock):
            kt_ref[0, hd, kb] = kht[:, kb * kblock:(kb + 1) * kblock].astype(BF16)

    v = jnp.dot(xn, wv_ref[...], preferred_element_type=F32).astype(BF16)
    for hd in range(HEADS):
        v_ref[0, hd] = v[:, hd * HEAD_DIM:(hd + 1) * HEAD_DIM]

    f = jnp.dot(xn, wf_ref[...], preferred_element_type=F32) + bf_ref[...]
    cum = jnp.minimum(f, 0.0) - jnp.log1p(jnp.exp(-jnp.abs(f)))
    row = lax.broadcasted_iota(jnp.int32, (chunk, V7X_LANES), 0)
    k_step = 1
    while k_step < chunk:
        cum = cum + jnp.where(row >= k_step, pltpu.roll(cum, k_step, 0), 0.0)
        k_step *= 2
    cum = cum + ccar[0:1, :]
    last = n_valid - 1 if n_valid < chunk else chunk - 1
    ccar[...] = jnp.broadcast_to(cum[last:last + 1, :], (V7X_SUBLANES, V7X_LANES))

    hi = cum.astype(BF16).astype(F32)
    mid = (cum - hi).astype(BF16).astype(F32)
    lo = (cum - hi - mid).astype(BF16).astype(F32)
    sub = lax.broadcasted_iota(jnp.int32, (chunk, V7X_LANES), 1) & (AUG_LANES - 1)
    one = jnp.ones_like(cum)
    zero = jnp.zeros_like(cum)
    qa = jnp.where(sub < 3, one, jnp.where(sub == 3, hi, jnp.where(sub == 4, mid,
                                                                  jnp.where(sub == 5, lo, zero))))
    ka = jnp.where(sub == 0, -hi, jnp.where(sub == 1, -mid, jnp.where(sub == 2, -lo,
                                                                      jnp.where(sub < 6, one, zero))))
    if n_valid < chunk:
        ka = jnp.where((row >= n_valid) & (sub == 0), MASKED, ka)
    qa_ref[0] = qa.astype(BF16)
    kat = ka.T
    for kb in range(chunk // kblock):
        kat_ref[0, kb] = kat[:, kb * kblock:(kb + 1) * kblock].astype(BF16)

    @pl.when(c == pl.num_programs(1) - 1)
    def _():
        cfin_ref[0] = ccar[...]


def _kv_proj(h, c0, p, *, chunk, kblock, n_valid):
    bsz, t, d = h.shape
    assert t % chunk == 0 and chunk % kblock == 0
    assert n_valid == t or t == chunk
    nkb = t // kblock
    const2 = lambda b, c: (0, 0)
    cpk = chunk // kblock
    outs = pl.pallas_call(
        functools.partial(_kv_kernel, chunk=chunk, kblock=kblock, n_valid=n_valid),
        grid=(bsz, t // chunk),
        in_specs=[
            pl.BlockSpec((1, chunk, d), lambda b, c: (b, c, 0)),
            pl.BlockSpec((V7X_SUBLANES, V7X_LANES), const2),
            pl.BlockSpec((1, d), const2),
            pl.BlockSpec((d, ATTN_WIDTH), const2),
            pl.BlockSpec((d, ATTN_WIDTH), const2),
            pl.BlockSpec((d, V7X_LANES), const2),
            pl.BlockSpec((1, V7X_LANES), const2),
            pl.BlockSpec((1, HEAD_DIM), const2),
        ],
        out_specs=[
            pl.BlockSpec((1, HEADS, cpk, HEAD_DIM, kblock), lambda b, c: (b, 0, c, 0, 0)),
            pl.BlockSpec((1, cpk, V7X_LANES, kblock), lambda b, c: (b, c, 0, 0)),
            pl.BlockSpec((1, HEADS, chunk, HEAD_DIM), lambda b, c: (b, 0, c, 0)),
            pl.BlockSpec((1, chunk, V7X_LANES), lambda b, c: (b, c, 0)),
            pl.BlockSpec((1, V7X_SUBLANES, V7X_LANES), lambda b, c: (b, 0, 0)),
        ],
        out_shape=[
            jax.ShapeDtypeStruct((bsz, HEADS, nkb, HEAD_DIM, kblock), BF16),
            jax.ShapeDtypeStruct((bsz, nkb, V7X_LANES, kblock), BF16),
            jax.ShapeDtypeStruct((bsz, HEADS, t, HEAD_DIM), BF16),
            jax.ShapeDtypeStruct((bsz, t, V7X_LANES), BF16),
            jax.ShapeDtypeStruct((bsz, V7X_SUBLANES, V7X_LANES), F32),
        ],
        scratch_shapes=[pltpu.VMEM((V7X_SUBLANES, V7X_LANES), F32)],
        compiler_params=pltpu.CompilerParams(
            dimension_semantics=("arbitrary", "arbitrary"), vmem_limit_bytes=VMEM_LIMIT),
        name="shared_kv" if n_valid == t else "shared_kv_meta",
    )(h, c0, p["gn"], p["wk"], p["wv"], p["wf"], p["bf"], p["gk"])
    return outs


def _b_layer_kernel(x_ref, qa_ref, kt_ref, kat_ref, v_ref, ktm_ref, katm_ref, vm_ref,
                    gn_ref, win_ref, gq_ref, wout_ref, out_ref,
                    qg_s, q_s, o_s, m_s, l_s, *, blk):
    qi = pl.program_id(1)
    aw = ATTN_WIDTH
    scale = HEAD_DIM ** -0.5

    x = x_ref[0]
    xn = (x * _rms_scale(x) * gn_ref[...]).astype(BF16)
    qg_s[...] = jnp.dot(xn, win_ref[...], preferred_element_type=F32)

    qa = qa_ref[0]
    lane_head = lax.shift_right_logical(
        lax.broadcasted_iota(jnp.int32, (blk, V7X_LANES), 1), AUG_LANES.bit_length() - 1)
    for hd in range(HEADS):
        qh = qg_s[:, hd * HEAD_DIM:(hd + 1) * HEAD_DIM]
        qn = qh * _rms_scale(qh) * (gq_ref[...] * scale)
        q_s[hd, :, 0:HEAD_DIM] = qn.astype(BF16)
        q_s[hd, :, HEAD_DIM:2 * HEAD_DIM] = jnp.where(lane_head == hd, qa, jnp.zeros_like(qa))

    rows = lax.broadcasted_iota(jnp.int32, (blk, blk), 0)
    cols = lax.broadcasted_iota(jnp.int32, (blk, blk), 1)
    causal = cols <= rows

    def online_update(hd, s, v_blk):
        m_prev = m_s[hd]
        m_next = jnp.maximum(m_prev, jnp.max(s, axis=1, keepdims=True))
        alpha = jnp.exp(m_prev - m_next)
        p = jnp.concatenate(
            [jnp.exp(s[:, j * V7X_LANES:(j + 1) * V7X_LANES] - m_next)
             for j in range(s.shape[1] // V7X_LANES)], axis=1)
        l_s[hd] = alpha * l_s[hd] + jnp.sum(p, axis=1, keepdims=True)
        o_s[hd] = alpha * o_s[hd] + jnp.dot(p.astype(BF16), v_blk, preferred_element_type=F32)
        m_s[hd] = m_next

    katm = katm_ref[0, 0]
    for hd in range(HEADS):
        ktm = jnp.concatenate([ktm_ref[0, hd, 0], katm], axis=0)
        s = jnp.dot(q_s[hd], ktm, preferred_element_type=F32)
        m0 = jnp.max(s, axis=1, keepdims=True)
        p = jnp.exp(s - m0)
        m_s[hd] = jnp.broadcast_to(m0, (blk, V7X_LANES))
        l_s[hd] = jnp.broadcast_to(jnp.sum(p, axis=1, keepdims=True), (blk, V7X_LANES))
        o_s[hd] = jnp.dot(p.astype(BF16), vm_ref[0, hd], preferred_element_type=F32)

    def kv_body(kj, carry):
        kat = kat_ref[0, kj]
        r0 = pl.multiple_of(kj * blk, blk)
        for hd in range(HEADS):
            kt = jnp.concatenate([kt_ref[0, hd, kj], kat], axis=0)
            s = jnp.dot(q_s[hd], kt, preferred_element_type=F32)
            online_update(hd, s, v_ref[0, hd, pl.ds(r0, blk), :])
        return carry

    lax.fori_loop(0, qi, kv_body, 0)

    kat = kat_ref[0, qi]
    r0 = pl.multiple_of(qi * blk, blk)
    for hd in range(HEADS):
        kt = jnp.concatenate([kt_ref[0, hd, qi], kat], axis=0)
        s = jnp.dot(q_s[hd], kt, preferred_element_type=F32)
        online_update(hd, jnp.where(causal, s, MASKED), v_ref[0, hd, pl.ds(r0, blk), :])

    ys = []
    for hd in range(HEADS):
        gate = qg_s[:, aw + hd * HEAD_DIM:aw + (hd + 1) * HEAD_DIM]
        ys.append((o_s[hd] / l_s[hd] * (gate * _sigmoid(gate))).astype(BF16))
    y = jnp.concatenate(ys, axis=1)
    out_ref[0] = x + jnp.dot(y, wout_ref[...], preferred_element_type=F32)


def _b_layer(x, qa, kt, kat, v, ktm, katm, vm, p, *, blk):
    bsz, t, d = x.shape
    assert t % blk == 0
    nkb = t // blk
    aw = ATTN_WIDTH
    const2 = lambda b, q: (0, 0)
    return pl.pallas_call(
        functools.partial(_b_layer_kernel, blk=blk),
        grid=(bsz, t // blk),
        in_specs=[
            pl.BlockSpec((1, blk, d), lambda b, q: (b, q, 0)),
            pl.BlockSpec((1, blk, V7X_LANES), lambda b, q: (b, q, 0)),
            pl.BlockSpec((1, HEADS, nkb, HEAD_DIM, blk), lambda b, q: (b, 0, 0, 0, 0)),
            pl.BlockSpec((1, nkb, V7X_LANES, blk), lambda b, q: (b, 0, 0, 0)),
            pl.BlockSpec((1, HEADS, t, HEAD_DIM), lambda b, q: (b, 0, 0, 0)),
            pl.BlockSpec((1, HEADS, 1, HEAD_DIM, V7X_LANES), lambda b, q: (0, 0, 0, 0, 0)),
            pl.BlockSpec((1, 1, V7X_LANES, V7X_LANES), lambda b, q: (0, 0, 0, 0)),
            pl.BlockSpec((1, HEADS, V7X_LANES, HEAD_DIM), lambda b, q: (0, 0, 0, 0)),
            pl.BlockSpec((1, d), const2),
            pl.BlockSpec((d, 2 * aw), const2),
            pl.BlockSpec((1, HEAD_DIM), const2),
            pl.BlockSpec((aw, d), const2),
        ],
        out_specs=pl.BlockSpec((1, blk, d), lambda b, q: (b, q, 0)),
        out_shape=jax.ShapeDtypeStruct((bsz, t, d), F32),
        scratch_shapes=[
            pltpu.VMEM((blk, 2 * aw), F32),
            pltpu.VMEM((HEADS, blk, 2 * HEAD_DIM), BF16),
            pltpu.VMEM((HEADS, blk, HEAD_DIM), F32),
            pltpu.VMEM((HEADS, blk, V7X_LANES), F32),
            pltpu.VMEM((HEADS, blk, V7X_LANES), F32),
        ],
        compiler_params=pltpu.CompilerParams(
            dimension_semantics=("arbitrary", "arbitrary"), vmem_limit_bytes=VMEM_LIMIT),
        name="fox_attention_layer",
    )(x, qa, kt, kat, v, ktm, katm, vm, p["gn"], p["win"], p["gq"], p["wout"])


def _a_params(l, a_norm, a_w_in, a_conv_w, a_conv_b, a_w_r, a_b_r, a_w_i, a_b_i, a_lambda, a_w_out):
    wri = jnp.concatenate([a_w_r[l], a_w_i[l]], axis=-1).astype(BF16)
    bri = jnp.concatenate([a_b_r[l].reshape(LRU_BLOCKS, 1, LRU_BLOCK),
                           a_b_i[l].reshape(LRU_BLOCKS, 1, LRU_BLOCK)], axis=-1).astype(F32)
    return dict(gn=a_norm[l].reshape(1, -1).astype(F32), win=a_w_in[l].astype(BF16),
                cw=a_conv_w[l].astype(F32), cb=a_conv_b[l].reshape(1, -1).astype(F32),
                wri=wri, bri=bri, lam=a_lambda[l].reshape(1, -1).astype(F32),
                wout=a_w_out[l].astype(BF16))


def _kv_params(kv_norm, w_kv, b_f, k_norm):
    aw = ATTN_WIDTH
    pad = jnp.zeros((D_MODEL, AUG_LANES - 6), F32)
    wf = w_kv[:, 2 * aw:2 * aw + HEADS].astype(F32)
    wf = jnp.concatenate([jnp.concatenate([wf[:, h:h + 1]] * 6 + [pad], axis=1) for h in range(HEADS)],
                         axis=1)
    bfp = jnp.concatenate([jnp.concatenate([b_f[h:h + 1].astype(F32)] * 6 + [jnp.zeros((AUG_LANES - 6,), F32)])
                           for h in range(HEADS)]).reshape(1, V7X_LANES)
    return dict(gn=kv_norm.reshape(1, -1).astype(F32), wk=w_kv[:, 0:aw].astype(BF16),
                wv=w_kv[:, aw:2 * aw].astype(BF16), wf=wf.astype(BF16), bf=bfp,
                gk=k_norm.reshape(1, -1).astype(F32))


def kernel(x, meta_tokens, a_norm, a_w_in, a_conv_w, a_conv_b, a_w_r, a_b_r, a_w_i, a_b_i, a_lambda,
           a_w_out, kv_norm, w_kv, b_f, k_norm, b_norm, b_w_in, q_norm, b_w_out):
    n_a = a_norm.shape[0]
    n_b = b_norm.shape[0]
    h = x.astype(F32)
    hm = meta_tokens[None].astype(F32)
    zero_state = jnp.zeros((16, LRU_WIDTH), F32)

    for l in range(n_a):
        p = _a_params(l, a_norm, a_w_in, a_conv_w, a_conv_b, a_w_r, a_b_r, a_w_i, a_b_i, a_lambda,
                      a_w_out)
        hm, st = _a_layer(hm, zero_state, p, chunk=N_META, emit_state=True)
        h = _a_layer(h, st[0], p, chunk=A_CHUNK, emit_state=False)

    pkv = _kv_params(kv_norm, w_kv, b_f, k_norm)
    hm_pad = jnp.pad(hm, ((0, 0), (0, V7X_LANES - N_META), (0, 0)))
    c_zero = jnp.zeros((V7X_SUBLANES, V7X_LANES), F32)
    ktm, katm, vm, _, c_meta = _kv_proj(hm_pad, c_zero, pkv, chunk=V7X_LANES, kblock=V7X_LANES,
                                        n_valid=N_META)
    kt, kat, v, qa, _ = _kv_proj(h, c_meta[0], pkv, chunk=KV_CHUNK, kblock=ATT_BLOCK,
                                 n_valid=h.shape[1])

    for j in range(n_b):
        p = dict(gn=b_norm[j].reshape(1, -1).astype(F32), win=b_w_in[j].astype(BF16),
                 gq=q_norm[j].reshape(1, -1).astype(F32), wout=b_w_out[j].astype(BF16))
        h = _b_layer(h, qa, kt, kat, v, ktm, katm, vm, p, blk=ATT_BLOCK)
    return h
```

```python
import functools

import jax
import jax.numpy as jnp
from jax import lax
from jax.experimental import pallas as pl
from jax.experimental.pallas import tpu as pltpu

F32 = jnp.float32
BF16 = jnp.bfloat16

D_MODEL = 1024
LRU_WIDTH = 1024
LRU_BLOCKS = 4
LRU_BLOCK = LRU_WIDTH // LRU_BLOCKS
CONV_WIDTH = 4
LRU_C = 8.0
HEADS = 8
HEAD_DIM = 128
ATTN_WIDTH = HEADS * HEAD_DIM
N_META = 16
EPS = 1e-6

V7X_SUBLANES = 8
V7X_LANES = 128
AUG_LANES = V7X_LANES // HEADS
MASKED = -1e30
LOG2E = 1.4426950408889634
V_ROWS = HEAD_DIM + 16

A_CHUNK = 512
KV_CHUNK = 512
ATT_BLOCK = 256
VMEM_LIMIT = 56 * 1024 * 1024


def _rms_scale(x):
    return lax.rsqrt(jnp.mean(x * x, axis=-1, keepdims=True) + EPS)


def _sigmoid(x):
    return 1.0 / (1.0 + jnp.exp(-x))


def _softplus(x):
    return jnp.maximum(x, 0.0) + jnp.log1p(jnp.exp(-jnp.abs(x)))


def _a_layer_kernel(x_ref, st0_ref, gn_ref, win_ref, cw_ref, cb_ref, wri_ref, bri_ref, lam_ref,
                    wout_ref, out_ref, *rest, chunk, emit_state):
    if emit_state:
        st_out_ref, ug_s, ubuf, a_s, b_s, hcar = rest
    else:
        ug_s, ubuf, a_s, b_s, hcar = rest
    c = pl.program_id(1)
    w = LRU_WIDTH

    @pl.when(c == 0)
    def _():
        hcar[...] = st0_ref[0:8, :]
        ubuf[0:8, :] = st0_ref[8:16, :]

    x = x_ref[0]
    xn = (x * _rms_scale(x) * gn_ref[...]).astype(BF16)
    ug_s[...] = jnp.dot(xn, win_ref[...], preferred_element_type=F32)

    u = ug_s[:, 0:w]
    ubuf[8:8 + chunk, :] = u
    uc = (cb_ref[...] + cw_ref[0:1, :] * ubuf[5:5 + chunk, :] + cw_ref[1:2, :] * ubuf[6:6 + chunk, :]
          + cw_ref[2:3, :] * ubuf[7:7 + chunk, :] + cw_ref[3:4, :] * u)
    ubuf[0:8, :] = ubuf[chunk:chunk + 8, :]

    ucb = uc.astype(BF16)
    neg_c_sp = -LRU_C * _softplus(-lam_ref[...])
    for n in range(LRU_BLOCKS):
        lo, hi = n * LRU_BLOCK, (n + 1) * LRU_BLOCK
        ri = jnp.dot(ucb[:, lo:hi], wri_ref[n], preferred_element_type=F32) + bri_ref[n]
        r = _sigmoid(ri[:, 0:LRU_BLOCK])
        i = _sigmoid(ri[:, LRU_BLOCK:2 * LRU_BLOCK])
        a = jnp.exp(r * neg_c_sp[:, lo:hi])
        a_s[:, lo:hi] = a
        b_s[:, lo:hi] = jnp.sqrt(1.0 - a * a) * i * uc[:, lo:hi]

    row = lax.broadcasted_iota(jnp.int32, (V7X_SUBLANES, w), 0)

    def tile_body(j, hprev):
        r0 = pl.multiple_of(j * V7X_SUBLANES, V7X_SUBLANES)
        a = a_s[pl.ds(r0, V7X_SUBLANES), :]
        b = b_s[pl.ds(r0, V7X_SUBLANES), :]
        for k in (1, 2, 4):
            keep = row >= k
            a_sh = jnp.where(keep, pltpu.roll(a, k, 0), 1.0)
            b_sh = jnp.where(keep, pltpu.roll(b, k, 0), 0.0)
            b = a * b_sh + b
            a = a * a_sh
        h = a * hprev + b
        b_s[pl.ds(r0, V7X_SUBLANES), :] = h
        return h[V7X_SUBLANES - 1:V7X_SUBLANES, :]

    hlast = lax.fori_loop(0, chunk // V7X_SUBLANES, tile_body, hcar[0:1, :], unroll=2)
    hcar[...] = jnp.broadcast_to(hlast, (V7X_SUBLANES, w))

    gate = ug_s[:, w:2 * w]
    y = (b_s[...] * (gate * _sigmoid(gate))).astype(BF16)
    out_ref[0] = x + jnp.dot(y, wout_ref[...], preferred_element_type=F32)

    if emit_state:
        @pl.when(c == pl.num_programs(1) - 1)
        def _():
            st_out_ref[0, 0:8, :] = hcar[...]
            st_out_ref[0, 8:16, :] = ubuf[0:8, :]


def _a_layer(x, st0, p, *, chunk, emit_state):
    bsz, t, d = x.shape
    assert t % chunk == 0 and chunk % V7X_SUBLANES == 0 and chunk >= 8
    w = LRU_WIDTH
    const2 = lambda b, c: (0, 0)
    const3 = lambda b, c: (0, 0, 0)
    in_specs = [
        pl.BlockSpec((1, chunk, d), lambda b, c: (b, c, 0)),
        pl.BlockSpec((16, w), const2),
        pl.BlockSpec((1, d), const2),
        pl.BlockSpec((d, 2 * w), const2),
        pl.BlockSpec((CONV_WIDTH, w), const2),
        pl.BlockSpec((1, w), const2),
        pl.BlockSpec((LRU_BLOCKS, LRU_BLOCK, 2 * LRU_BLOCK), const3),
        pl.BlockSpec((LRU_BLOCKS, 1, 2 * LRU_BLOCK), const3),
        pl.BlockSpec((1, w), const2),
        pl.BlockSpec((w, d), const2),
    ]
    out_shape = [jax.ShapeDtypeStruct((bsz, t, d), F32)]
    out_specs = [pl.BlockSpec((1, chunk, d), lambda b, c: (b, c, 0))]
    if emit_state:
        out_shape.append(jax.ShapeDtypeStruct((bsz, 16, w), F32))
        out_specs.append(pl.BlockSpec((1, 16, w), lambda b, c: (b, 0, 0)))
    scratch = [
        pltpu.VMEM((chunk, 2 * w), F32),
        pltpu.VMEM((chunk + 8, w), F32),
        pltpu.VMEM((chunk, w), F32),
        pltpu.VMEM((chunk, w), F32),
        pltpu.VMEM((V7X_SUBLANES, w), F32),
    ]
    res = pl.pallas_call(
        functools.partial(_a_layer_kernel, chunk=chunk, emit_state=emit_state),
        grid=(bsz, t // chunk),
        in_specs=in_specs,
        out_specs=out_specs,
        out_shape=out_shape,
        scratch_shapes=scratch,
        compiler_params=pltpu.CompilerParams(
            dimension_semantics=("arbitrary", "arbitrary"), vmem_limit_bytes=VMEM_LIMIT),
        name="rglru_layer_state" if emit_state else "rglru_layer",
    )(x, st0, p["gn"], p["win"], p["cw"], p["cb"], p["wri"], p["bri"], p["lam"], p["wout"])
    return res if emit_state else res[0]


def _kv_kernel(h_ref, c0_ref, gn_ref, wk_ref, wv_ref, wf_ref, bf_ref, gk_ref,
               k_ref, ka_ref, vt_ref, qa_ref, cfin_ref, ccar, *, chunk, kblock, n_valid):
    c = pl.program_id(1)

    @pl.when(c == 0)
    def _():
        ccar[...] = c0_ref[...]

    x = h_ref[0]
    xn = (x * _rms_scale(x) * gn_ref[...]).astype(BF16)

    k = jnp.dot(xn, wk_ref[...], preferred_element_type=F32)
    for hd in range(HEADS):
        kh = k[:, hd * HEAD_DIM:(hd + 1) * HEAD_DIM]
        k_ref[0, hd] = (kh * _rms_scale(kh) * gk_ref[...]).astype(BF16)

    v = jnp.dot(xn, wv_ref[...], preferred_element_type=F32)
    tail_row = lax.broadcasted_iota(jnp.int32, (V_ROWS - HEAD_DIM, kblock), 0)
    ones_rows = jnp.where(tail_row == 0, 1.0, 0.0).astype(BF16)
    for hd in range(HEADS):
        vt = v[:, hd * HEAD_DIM:(hd + 1) * HEAD_DIM].T
        for kb in range(chunk // kblock):
            vt_ref[0, hd, kb, 0:HEAD_DIM, :] = vt[:, kb * kblock:(kb + 1) * kblock].astype(BF16)
            vt_ref[0, hd, kb, HEAD_DIM:V_ROWS, :] = ones_rows

    f = jnp.dot(xn, wf_ref[...], preferred_element_type=F32) + bf_ref[...]
    cum = jnp.minimum(f, 0.0) - jnp.log1p(jnp.exp(-jnp.abs(f)))
    row = lax.broadcasted_iota(jnp.int32, (chunk, V7X_LANES), 0)
    k_step = 1
    while k_step < chunk:
        cum = cum + jnp.where(row >= k_step, pltpu.roll(cum, k_step, 0), 0.0)
        k_step *= 2
    cum = cum + ccar[0:1, :]
    last = n_valid - 1 if n_valid < chunk else chunk - 1
    ccar[...] = jnp.broadcast_to(cum[last:last + 1, :], (V7X_SUBLANES, V7X_LANES))

    cum2 = cum * LOG2E
    hi = cum2.astype(BF16).astype(F32)
    mid = (cum2 - hi).astype(BF16).astype(F32)
    lo = (cum2 - hi - mid).astype(BF16).astype(F32)
    sub = lax.broadcasted_iota(jnp.int32, (chunk, V7X_LANES), 1) & (AUG_LANES - 1)
    one = jnp.ones_like(cum)
    zero = jnp.zeros_like(cum)
    qa = jnp.where(sub < 3, one, jnp.where(sub == 3, hi, jnp.where(sub == 4, mid,
                                                                  jnp.where(sub == 5, lo, zero))))
    ka = jnp.where(sub == 0, -hi, jnp.where(sub == 1, -mid, jnp.where(sub == 2, -lo,
                                                                      jnp.where(sub < 6, one, zero))))
    if n_valid < chunk:
        ka = jnp.where((row >= n_valid) & (sub == 0), MASKED, ka)
    qa_ref[0] = qa.astype(BF16)
    ka_ref[0] = ka.astype(BF16)

    @pl.when(c == pl.num_programs(1) - 1)
    def _():
        cfin_ref[0] = ccar[...]


def _kv_proj(h, c0, p, *, chunk, kblock, n_valid):
    bsz, t, d = h.shape
    assert t % chunk == 0 and chunk % kblock == 0
    assert n_valid == t or t == chunk
    nkb = t // kblock
    const2 = lambda b, c: (0, 0)
    cpk = chunk // kblock
    outs = pl.pallas_call(
        functools.partial(_kv_kernel, chunk=chunk, kblock=kblock, n_valid=n_valid),
        grid=(bsz, t // chunk),
        in_specs=[
            pl.BlockSpec((1, chunk, d), lambda b, c: (b, c, 0)),
            pl.BlockSpec((V7X_SUBLANES, V7X_LANES), const2),
            pl.BlockSpec((1, d), const2),
            pl.BlockSpec((d, ATTN_WIDTH), const2),
            pl.BlockSpec((d, ATTN_WIDTH), const2),
            pl.BlockSpec((d, V7X_LANES), const2),
            pl.BlockSpec((1, V7X_LANES), const2),
            pl.BlockSpec((1, HEAD_DIM), const2),
        ],
        out_specs=[
            pl.BlockSpec((1, HEADS, chunk, HEAD_DIM), lambda b, c: (b, 0, c, 0)),
            pl.BlockSpec((1, chunk, V7X_LANES), lambda b, c: (b, c, 0)),
            pl.BlockSpec((1, HEADS, cpk, V_ROWS, kblock), lambda b, c: (b, 0, c, 0, 0)),
            pl.BlockSpec((1, chunk, V7X_LANES), lambda b, c: (b, c, 0)),
            pl.BlockSpec((1, V7X_SUBLANES, V7X_LANES), lambda b, c: (b, 0, 0)),
        ],
        out_shape=[
            jax.ShapeDtypeStruct((bsz, HEADS, t, HEAD_DIM), BF16),
            jax.ShapeDtypeStruct((bsz, t, V7X_LANES), BF16),
            jax.ShapeDtypeStruct((bsz, HEADS, nkb, V_ROWS, kblock), BF16),
            jax.ShapeDtypeStruct((bsz, t, V7X_LANES), BF16),
            jax.ShapeDtypeStruct((bsz, V7X_SUBLANES, V7X_LANES), F32),
        ],
        scratch_shapes=[pltpu.VMEM((V7X_SUBLANES, V7X_LANES), F32)],
        compiler_params=pltpu.CompilerParams(
            dimension_semantics=("arbitrary", "arbitrary"), vmem_limit_bytes=VMEM_LIMIT),
        name="shared_kv" if n_valid == t else "shared_kv_meta",
    )(h, c0, p["gn"], p["wk"], p["wv"], p["wf"], p["bf"], p["gk"])
    return outs


def _b_layer_kernel(x_ref, qa_ref, k_ref, ka_ref, vt_ref, km_ref, kam_ref, vtm_ref,
                    gn_ref, win_ref, gq_ref, wout_ref, out_ref,
                    qg_s, qt_s, ot_s, m_s, *, blk):
    qi = pl.program_id(1)
    aw = ATTN_WIDTH
    scale = HEAD_DIM ** -0.5 * LOG2E

    x = x_ref[0]
    xn = (x * _rms_scale(x) * gn_ref[...]).astype(BF16)
    qg_s[...] = jnp.dot(xn, win_ref[...], preferred_element_type=F32)

    qa = qa_ref[0].astype(F32)
    lane_head = lax.shift_right_logical(
        lax.broadcasted_iota(jnp.int32, (blk, V7X_LANES), 1), AUG_LANES.bit_length() - 1)
    for hd in range(HEADS):
        qh = qg_s[:, hd * HEAD_DIM:(hd + 1) * HEAD_DIM]
        qn = qh * _rms_scale(qh) * (gq_ref[...] * scale)
        qt_s[hd, 0:HEAD_DIM, :] = qn.T.astype(BF16)
        qt_s[hd, HEAD_DIM:2 * HEAD_DIM, :] = jnp.where(lane_head == hd, qa, 0.0).T.astype(BF16)

    def online_update(hd, s, vt_blk):
        m_prev = m_s[hd]
        m_next = jnp.maximum(m_prev, jnp.max(s, axis=0, keepdims=True))
        alpha = jnp.exp2(m_prev - m_next)
        p = jnp.exp2(s - m_next).astype(BF16)
        ot_s[hd] = alpha * ot_s[hd] + jnp.dot(vt_blk, p, preferred_element_type=F32)
        m_s[hd] = m_next

    kam = kam_ref[0]
    for hd in range(HEADS):
        s = jnp.dot(jnp.concatenate([km_ref[0, hd], kam], axis=1), qt_s[hd],
                    preferred_element_type=F32)
        m0 = jnp.max(s, axis=0, keepdims=True)
        m_s[hd] = m0
        ot_s[hd] = jnp.dot(vtm_ref[0, hd, 0], jnp.exp2(s - m0).astype(BF16),
                           preferred_element_type=F32)

    def scores(hd, kj):
        r0 = pl.multiple_of(kj * blk, blk)
        k_aug = jnp.concatenate([k_ref[0, hd, pl.ds(r0, blk), :], ka_ref[0, pl.ds(r0, blk), :]],
                                axis=1)
        return jnp.dot(k_aug, qt_s[hd], preferred_element_type=F32)

    def kv_body(kj, carry):
        ss = [scores(hd, kj) for hd in range(HEADS)]
        for hd in range(HEADS):
            online_update(hd, ss[hd], vt_ref[0, hd, kj])
        return carry

    lax.fori_loop(0, qi, kv_body, 0)

    key_row = lax.broadcasted_iota(jnp.int32, (blk, blk), 0)
    query_col = lax.broadcasted_iota(jnp.int32, (blk, blk), 1)
    causal = key_row <= query_col
    ss = [scores(hd, qi) for hd in range(HEADS)]
    for hd in range(HEADS):
        online_update(hd, jnp.where(causal, ss[hd], MASKED), vt_ref[0, hd, qi])

    ys = []
    for hd in range(HEADS):
        gate = qg_s[:, aw + hd * HEAD_DIM:aw + (hd + 1) * HEAD_DIM]
        o = (ot_s[hd, 0:HEAD_DIM, :] / ot_s[hd, HEAD_DIM:HEAD_DIM + 1, :]).T
        ys.append((o * (gate * _sigmoid(gate))).astype(BF16))
    y = jnp.concatenate(ys, axis=1)
    out_ref[0] = x + jnp.dot(y, wout_ref[...], preferred_element_type=F32)


def _b_layer(x, qa, k, ka, vt, km, kam, vtm, p, *, blk):
    bsz, t, d = x.shape
    assert t % blk == 0
    nkb = t // blk
    aw = ATTN_WIDTH
    const2 = lambda b, q: (0, 0)
    return pl.pallas_call(
        functools.partial(_b_layer_kernel, blk=blk),
        grid=(bsz, t // blk),
        in_specs=[
            pl.BlockSpec((1, blk, d), lambda b, q: (b, q, 0)),
            pl.BlockSpec((1, blk, V7X_LANES), lambda b, q: (b, q, 0)),
            pl.BlockSpec((1, HEADS, t, HEAD_DIM), lambda b, q: (b, 0, 0, 0)),
            pl.BlockSpec((1, t, V7X_LANES), lambda b, q: (b, 0, 0)),
            pl.BlockSpec((1, HEADS, nkb, V_ROWS, blk), lambda b, q: (b, 0, 0, 0, 0)),
            pl.BlockSpec((1, HEADS, V7X_LANES, HEAD_DIM), lambda b, q: (0, 0, 0, 0)),
            pl.BlockSpec((1, V7X_LANES, V7X_LANES), lambda b, q: (0, 0, 0)),
            pl.BlockSpec((1, HEADS, 1, V_ROWS, V7X_LANES), lambda b, q: (0, 0, 0, 0, 0)),
            pl.BlockSpec((1, d), const2),
            pl.BlockSpec((d, 2 * aw), const2),
            pl.BlockSpec((1, HEAD_DIM), const2),
            pl.BlockSpec((aw, d), const2),
        ],
        out_specs=pl.BlockSpec((1, blk, d), lambda b, q: (b, q, 0)),
        out_shape=jax.ShapeDtypeStruct((bsz, t, d), F32),
        scratch_shapes=[
            pltpu.VMEM((blk, 2 * aw), F32),
            pltpu.VMEM((HEADS, 2 * HEAD_DIM, blk), BF16),
            pltpu.VMEM((HEADS, V_ROWS, blk), F32),
            pltpu.VMEM((HEADS, 1, blk), F32),
        ],
        compiler_params=pltpu.CompilerParams(
            dimension_semantics=("arbitrary", "arbitrary"), vmem_limit_bytes=VMEM_LIMIT),
        name="fox_attention_layer",
    )(x, qa, k, ka, vt, km, kam, vtm, p["gn"], p["win"], p["gq"], p["wout"])


def _a_params(l, a_norm, a_w_in, a_conv_w, a_conv_b, a_w_r, a_b_r, a_w_i, a_b_i, a_lambda, a_w_out):
    wri = jnp.concatenate([a_w_r[l], a_w_i[l]], axis=-1).astype(BF16)
    bri = jnp.concatenate([a_b_r[l].reshape(LRU_BLOCKS, 1, LRU_BLOCK),
                           a_b_i[l].reshape(LRU_BLOCKS, 1, LRU_BLOCK)], axis=-1).astype(F32)
    return dict(gn=a_norm[l].reshape(1, -1).astype(F32), win=a_w_in[l].astype(BF16),
                cw=a_conv_w[l].astype(F32), cb=a_conv_b[l].reshape(1, -1).astype(F32),
                wri=wri, bri=bri, lam=a_lambda[l].reshape(1, -1).astype(F32),
                wout=a_w_out[l].astype(BF16))


def _kv_params(kv_norm, w_kv, b_f, k_norm):
    aw = ATTN_WIDTH
    pad = jnp.zeros((D_MODEL, AUG_LANES - 6), F32)
    wf = w_kv[:, 2 * aw:2 * aw + HEADS].astype(F32)
    wf = jnp.concatenate([jnp.concatenate([wf[:, h:h + 1]] * 6 + [pad], axis=1) for h in range(HEADS)],
                         axis=1)
    bfp = jnp.concatenate([jnp.concatenate([b_f[h:h + 1].astype(F32)] * 6 + [jnp.zeros((AUG_LANES - 6,), F32)])
                           for h in range(HEADS)]).reshape(1, V7X_LANES)
    return dict(gn=kv_norm.reshape(1, -1).astype(F32), wk=w_kv[:, 0:aw].astype(BF16),
                wv=w_kv[:, aw:2 * aw].astype(BF16), wf=wf.astype(BF16), bf=bfp,
                gk=k_norm.reshape(1, -1).astype(F32))


def kernel(x, meta_tokens, a_norm, a_w_in, a_conv_w, a_conv_b, a_w_r, a_b_r, a_w_i, a_b_i, a_lambda,
           a_w_out, kv_norm, w_kv, b_f, k_norm, b_norm, b_w_in, q_norm, b_w_out):
    n_a = a_norm.shape[0]
    n_b = b_norm.shape[0]
    h = x.astype(F32)
    hm = meta_tokens[None].astype(F32)
    zero_state = jnp.zeros((16, LRU_WIDTH), F32)

    for l in range(n_a):
        p = _a_params(l, a_norm, a_w_in, a_conv_w, a_conv_b, a_w_r, a_b_r, a_w_i, a_b_i, a_lambda,
                      a_w_out)
        hm, st = _a_layer(hm, zero_state, p, chunk=N_META, emit_state=True)
        h = _a_layer(h, st[0], p, chunk=A_CHUNK, emit_state=False)

    pkv = _kv_params(kv_norm, w_kv, b_f, k_norm)
    hm_pad = jnp.pad(hm, ((0, 0), (0, V7X_LANES - N_META), (0, 0)))
    c_zero = jnp.zeros((V7X_SUBLANES, V7X_LANES), F32)
    km, kam, vtm, _, c_meta = _kv_proj(hm_pad, c_zero, pkv, chunk=V7X_LANES, kblock=V7X_LANES,
                                       n_valid=N_META)
    k, ka, vt, qa, _ = _kv_proj(h, c_meta[0], pkv, chunk=KV_CHUNK, kblock=ATT_BLOCK,
                                n_valid=h.shape[1])

    for j in range(n_b):
        p = dict(gn=b_norm[j].reshape(1, -1).astype(F32), win=b_w_in[j].astype(BF16),
                 gq=q_norm[j].reshape(1, -1).astype(F32), wout=b_w_out[j].astype(BF16))
        h = _b_layer(h, qa, k, ka, vt, km, kam, vtm, p, blk=ATT_BLOCK)
    return h
```

```python
import functools

import jax
import jax.numpy as jnp
from jax import lax
from jax.experimental import pallas as pl
from jax.experimental.pallas import tpu as pltpu

F32 = jnp.float32
BF16 = jnp.bfloat16

D_MODEL = 1024
LRU_WIDTH = 1024
LRU_BLOCKS = 4
LRU_BLOCK = LRU_WIDTH // LRU_BLOCKS
CONV_WIDTH = 4
LRU_C = 8.0
HEADS = 8
HEAD_DIM = 128
ATTN_WIDTH = HEADS * HEAD_DIM
N_META = 16
EPS = 1e-6

V7X_SUBLANES = 8
V7X_LANES = 128
AUG_LANES = V7X_LANES // HEADS
MASKED = -1e30
SQRT_FLOOR = 1e-37
LOG2E = 1.4426950408889634
V_ROWS = HEAD_DIM + 16

A_CHUNK = 512
KV_CHUNK = 512
ATT_BLOCK = 256
VMEM_LIMIT = 56 * 1024 * 1024


def _rms_scale(x):
    return lax.rsqrt(jnp.mean(x * x, axis=-1, keepdims=True) + EPS)


def _sigmoid(x):
    return jax.nn.sigmoid(x)


def _softplus(x):
    return jnp.maximum(x, 0.0) + jnp.log1p(jnp.exp(-jnp.abs(x)))


def _a_layer_kernel(x_ref, st0_ref, gn_ref, win_ref, cw_ref, cb_ref, wri_ref, bri_ref, lam_ref,
                    wout_ref, out_ref, *rest, chunk, emit_state):
    if emit_state:
        st_out_ref, ug_s, ubuf, hcar = rest
    else:
        ug_s, ubuf, hcar = rest
    c = pl.program_id(1)
    w = LRU_WIDTH

    @pl.when(c == 0)
    def _():
        hcar[...] = st0_ref[0:8, :]
        ubuf[...] = st0_ref[8:16, :]

    x = x_ref[0]
    xn = (x * _rms_scale(x) * gn_ref[...]).astype(BF16)
    ug_s[...] = jnp.dot(xn, win_ref[...], preferred_element_type=F32)

    u = ug_s[:, 0:w]
    tail = ubuf[...]
    row8 = lax.broadcasted_iota(jnp.int32, (V7X_SUBLANES, w), 0)

    def delayed(k):
        r = pltpu.roll(u, k, 0)
        top = jnp.where(row8 >= k, r[0:V7X_SUBLANES, :], pltpu.roll(tail, k, 0))
        return jnp.concatenate([top, r[V7X_SUBLANES:, :]], axis=0)

    uc = (cb_ref[...] + cw_ref[0:1, :] * delayed(3) + cw_ref[1:2, :] * delayed(2)
          + cw_ref[2:3, :] * delayed(1) + cw_ref[3:4, :] * u)
    ubuf[...] = u[chunk - V7X_SUBLANES:chunk, :]

    ucb = uc.astype(BF16)
    neg_c_sp = (-LRU_C * LOG2E) * _softplus(-lam_ref[...])
    tiles = chunk // V7X_SUBLANES
    row = lax.broadcasted_iota(jnp.int32, (1, V7X_SUBLANES, LRU_BLOCK), 1)
    ys = []
    for n in range(LRU_BLOCKS):
        lo, hi = n * LRU_BLOCK, (n + 1) * LRU_BLOCK
        ri = jnp.dot(ucb[:, lo:hi], wri_ref[n], preferred_element_type=F32) + bri_ref[n]
        r = _sigmoid(ri[:, 0:LRU_BLOCK])
        i = _sigmoid(ri[:, LRU_BLOCK:2 * LRU_BLOCK])
        a = jnp.exp2(r * neg_c_sp[:, lo:hi])
        om = 1.0 - a * a
        b = om * lax.rsqrt(jnp.maximum(om, SQRT_FLOOR)) * i * uc[:, lo:hi]
        a = a.reshape(tiles, V7X_SUBLANES, LRU_BLOCK)
        b = b.reshape(tiles, V7X_SUBLANES, LRU_BLOCK)
        a_in = jnp.where(row == 0, a, 0.0)
        a1 = jnp.where(row == 0, 0.0, a)
        a2 = a1 * pltpu.roll(a1, 1, 1)
        a4 = a2 * pltpu.roll(a2, 2, 1)
        carry = hcar[0:1, lo:hi]
        hs = []
        for j in range(tiles):
            h = b[j] + a_in[j] * carry
            h = a1[j] * pltpu.roll(h, 1, 0) + h
            h = a2[j] * pltpu.roll(h, 2, 0) + h
            h = a4[j] * pltpu.roll(h, 4, 0) + h
            hs.append(h)
            carry = h[V7X_SUBLANES - 1:V7X_SUBLANES, :]
        hcar[:, lo:hi] = jnp.broadcast_to(carry, (V7X_SUBLANES, LRU_BLOCK))
        gate = ug_s[:, w + lo:w + hi]
        ys.append((jnp.concatenate(hs, axis=0) * (gate * _sigmoid(gate))).astype(BF16))
    y = jnp.concatenate(ys, axis=1)
    out_ref[0] = x + jnp.dot(y, wout_ref[...], preferred_element_type=F32)

    if emit_state:
        @pl.when(c == pl.num_programs(1) - 1)
        def _():
            st_out_ref[0, 0:8, :] = hcar[...]
            st_out_ref[0, 8:16, :] = ubuf[...]


def _a_layer(x, st0, p, *, chunk, emit_state):
    bsz, t, d = x.shape
    assert t % chunk == 0 and chunk % V7X_SUBLANES == 0 and chunk >= 8
    w = LRU_WIDTH
    const2 = lambda b, c: (0, 0)
    const3 = lambda b, c: (0, 0, 0)
    in_specs = [
        pl.BlockSpec((1, chunk, d), lambda b, c: (b, c, 0)),
        pl.BlockSpec((16, w), const2),
        pl.BlockSpec((1, d), const2),
        pl.BlockSpec((d, 2 * w), const2),
        pl.BlockSpec((CONV_WIDTH, w), const2),
        pl.BlockSpec((1, w), const2),
        pl.BlockSpec((LRU_BLOCKS, LRU_BLOCK, 2 * LRU_BLOCK), const3),
        pl.BlockSpec((LRU_BLOCKS, 1, 2 * LRU_BLOCK), const3),
        pl.BlockSpec((1, w), const2),
        pl.BlockSpec((w, d), const2),
    ]
    out_shape = [jax.ShapeDtypeStruct((bsz, t, d), F32)]
    out_specs = [pl.BlockSpec((1, chunk, d), lambda b, c: (b, c, 0))]
    if emit_state:
        out_shape.append(jax.ShapeDtypeStruct((bsz, 16, w), F32))
        out_specs.append(pl.BlockSpec((1, 16, w), lambda b, c: (b, 0, 0)))
    scratch = [
        pltpu.VMEM((chunk, 2 * w), F32),
        pltpu.VMEM((V7X_SUBLANES, w), F32),
        pltpu.VMEM((V7X_SUBLANES, w), F32),
    ]
    res = pl.pallas_call(
        functools.partial(_a_layer_kernel, chunk=chunk, emit_state=emit_state),
        grid=(bsz, t // chunk),
        in_specs=in_specs,
        out_specs=out_specs,
        out_shape=out_shape,
        scratch_shapes=scratch,
        compiler_params=pltpu.CompilerParams(
            dimension_semantics=("arbitrary", "arbitrary"), vmem_limit_bytes=VMEM_LIMIT),
        name="rglru_layer_state" if emit_state else "rglru_layer",
    )(x, st0, p["gn"], p["win"], p["cw"], p["cb"], p["wri"], p["bri"], p["lam"], p["wout"])
    return res if emit_state else res[0]


def _kv_kernel(h_ref, c0_ref, gn_ref, wk_ref, wv_ref, wf_ref, bf_ref, gk_ref,
               k_ref, ka_ref, vt_ref, qa_ref, cfin_ref, ccar, *, chunk, kblock, n_valid):
    c = pl.program_id(1)

    @pl.when(c == 0)
    def _():
        ccar[...] = c0_ref[...]

    x = h_ref[0]
    xn = (x * _rms_scale(x) * gn_ref[...]).astype(BF16)

    k = jnp.dot(xn, wk_ref[...], preferred_element_type=F32)
    for hd in range(HEADS):
        kh = k[:, hd * HEAD_DIM:(hd + 1) * HEAD_DIM]
        k_ref[0, hd] = (kh * _rms_scale(kh) * gk_ref[...]).astype(BF16)

    v = jnp.dot(xn, wv_ref[...], preferred_element_type=F32)
    tail_row = lax.broadcasted_iota(jnp.int32, (V_ROWS - HEAD_DIM, kblock), 0)
    ones_rows = jnp.where(tail_row == 0, 1.0, 0.0).astype(BF16)
    for hd in range(HEADS):
        vt = v[:, hd * HEAD_DIM:(hd + 1) * HEAD_DIM].T
        for kb in range(chunk // kblock):
            vt_ref[0, hd, kb, 0:HEAD_DIM, :] = vt[:, kb * kblock:(kb + 1) * kblock].astype(BF16)
            vt_ref[0, hd, kb, HEAD_DIM:V_ROWS, :] = ones_rows

    f = jnp.dot(xn, wf_ref[...], preferred_element_type=F32) + bf_ref[...]
    cum = jnp.minimum(f, 0.0) - jnp.log1p(jnp.exp(-jnp.abs(f)))
    row = lax.broadcasted_iota(jnp.int32, (chunk, V7X_LANES), 0)
    k_step = 1
    while k_step < chunk:
        cum = cum + jnp.where(row >= k_step, pltpu.roll(cum, k_step, 0), 0.0)
        k_step *= 2
    cum = cum + ccar[0:1, :]
    last = n_valid - 1 if n_valid < chunk else chunk - 1
    ccar[...] = jnp.broadcast_to(cum[last:last + 1, :], (V7X_SUBLANES, V7X_LANES))

    cum2 = cum * LOG2E
    hi = cum2.astype(BF16).astype(F32)
    mid = (cum2 - hi).astype(BF16).astype(F32)
    lo = (cum2 - hi - mid).astype(BF16).astype(F32)
    sub = lax.broadcasted_iota(jnp.int32, (chunk, V7X_LANES), 1) & (AUG_LANES - 1)
    one = jnp.ones_like(cum)
    zero = jnp.zeros_like(cum)
    qa = jnp.where(sub < 3, one, jnp.where(sub == 3, hi, jnp.where(sub == 4, mid,
                                                                  jnp.where(sub == 5, lo, zero))))
    ka = jnp.where(sub == 0, -hi, jnp.where(sub == 1, -mid, jnp.where(sub == 2, -lo,
                                                                      jnp.where(sub < 6, one, zero))))
    if n_valid < chunk:
        ka = jnp.where((row >= n_valid) & (sub == 0), MASKED, ka)
    qa_ref[0] = qa.astype(BF16)
    ka_ref[0] = ka.astype(BF16)

    @pl.when(c == pl.num_programs(1) - 1)
    def _():
        cfin_ref[0] = ccar[...]


def _kv_proj(h, c0, p, *, chunk, kblock, n_valid):
    bsz, t, d = h.shape
    assert t % chunk == 0 and chunk % kblock == 0
    assert n_valid == t or t == chunk
    nkb = t // kblock
    const2 = lambda b, c: (0, 0)
    cpk = chunk // kblock
    outs = pl.pallas_call(
        functools.partial(_kv_kernel, chunk=chunk, kblock=kblock, n_valid=n_valid),
        grid=(bsz, t // chunk),
        in_specs=[
            pl.BlockSpec((1, chunk, d), lambda b, c: (b, c, 0)),
            pl.BlockSpec((V7X_SUBLANES, V7X_LANES), const2),
            pl.BlockSpec((1, d), const2),
            pl.BlockSpec((d, ATTN_WIDTH), const2),
            pl.BlockSpec((d, ATTN_WIDTH), const2),
            pl.BlockSpec((d, V7X_LANES), const2),
            pl.BlockSpec((1, V7X_LANES), const2),
            pl.BlockSpec((1, HEAD_DIM), const2),
        ],
        out_specs=[
            pl.BlockSpec((1, HEADS, chunk, HEAD_DIM), lambda b, c: (b, 0, c, 0)),
            pl.BlockSpec((1, chunk, V7X_LANES), lambda b, c: (b, c, 0)),
            pl.BlockSpec((1, HEADS, cpk, V_ROWS, kblock), lambda b, c: (b, 0, c, 0, 0)),
            pl.BlockSpec((1, chunk, V7X_LANES), lambda b, c: (b, c, 0)),
            pl.BlockSpec((1, V7X_SUBLANES, V7X_LANES), lambda b, c: (b, 0, 0)),
        ],
        out_shape=[
            jax.ShapeDtypeStruct((bsz, HEADS, t, HEAD_DIM), BF16),
            jax.ShapeDtypeStruct((bsz, t, V7X_LANES), BF16),
            jax.ShapeDtypeStruct((bsz, HEADS, nkb, V_ROWS, kblock), BF16),
            jax.ShapeDtypeStruct((bsz, t, V7X_LANES), BF16),
            jax.ShapeDtypeStruct((bsz, V7X_SUBLANES, V7X_LANES), F32),
        ],
        scratch_shapes=[pltpu.VMEM((V7X_SUBLANES, V7X_LANES), F32)],
        compiler_params=pltpu.CompilerParams(
            dimension_semantics=("arbitrary", "arbitrary"), vmem_limit_bytes=VMEM_LIMIT),
        name="shared_kv" if n_valid == t else "shared_kv_meta",
    )(h, c0, p["gn"], p["wk"], p["wv"], p["wf"], p["bf"], p["gk"])
    return outs


def _b_layer_kernel(x_ref, qa_ref, k_ref, ka_ref, vt_ref, km_ref, kam_ref, vtm_ref,
                    gn_ref, win_ref, gq_ref, wout_ref, out_ref,
                    qg_s, qt_s, ot_s, m_s, *, blk):
    qi = pl.program_id(1)
    aw = ATTN_WIDTH
    scale = HEAD_DIM ** -0.5 * LOG2E

    x = x_ref[0]
    xn = (x * _rms_scale(x) * gn_ref[...]).astype(BF16)
    qg_s[...] = jnp.dot(xn, win_ref[...], preferred_element_type=F32)

    qa = qa_ref[0].astype(F32)
    lane_head = lax.shift_right_logical(
        lax.broadcasted_iota(jnp.int32, (blk, V7X_LANES), 1), AUG_LANES.bit_length() - 1)
    for hd in range(HEADS):
        qh = qg_s[:, hd * HEAD_DIM:(hd + 1) * HEAD_DIM]
        qn = qh * _rms_scale(qh) * (gq_ref[...] * scale)
        qt_s[hd, 0:HEAD_DIM, :] = qn.T.astype(BF16)
        qt_s[hd, HEAD_DIM:2 * HEAD_DIM, :] = jnp.where(lane_head == hd, qa, 0.0).T.astype(BF16)

    def online_update(hd, s, vt_blk):
        m_prev = m_s[hd]
        m_next = jnp.maximum(m_prev, jnp.max(s, axis=0, keepdims=True))
        alpha = jnp.exp2(m_prev - m_next)
        p = jnp.exp2(s - m_next).astype(BF16)
        ot_s[hd] = alpha * ot_s[hd] + jnp.dot(vt_blk, p, preferred_element_type=F32)
        m_s[hd] = m_next

    kam = kam_ref[0]
    for hd in range(HEADS):
        s = jnp.dot(jnp.concatenate([km_ref[0, hd], kam], axis=1), qt_s[hd],
                    preferred_element_type=F32)
        m0 = jnp.max(s, axis=0, keepdims=True)
        m_s[hd] = m0
        ot_s[hd] = jnp.dot(vtm_ref[0, hd, 0], jnp.exp2(s - m0).astype(BF16),
                           preferred_element_type=F32)

    def scores(hd, kj):
        r0 = pl.multiple_of(kj * blk, blk)
        k_aug = jnp.concatenate([k_ref[0, hd, pl.ds(r0, blk), :], ka_ref[0, pl.ds(r0, blk), :]],
                                axis=1)
        return jnp.dot(k_aug, qt_s[hd], preferred_element_type=F32)

    def kv_body(kj, carry):
        ss = [scores(hd, kj) for hd in range(HEADS)]
        for hd in range(HEADS):
            online_update(hd, ss[hd], vt_ref[0, hd, kj])
        return carry

    lax.fori_loop(0, qi, kv_body, 0)

    key_row = lax.broadcasted_iota(jnp.int32, (blk, blk), 0)
    query_col = lax.broadcasted_iota(jnp.int32, (blk, blk), 1)
    causal = key_row <= query_col
    ss = [scores(hd, qi) for hd in range(HEADS)]
    for hd in range(HEADS):
        online_update(hd, jnp.where(causal, ss[hd], MASKED), vt_ref[0, hd, qi])

    ys = []
    for hd in range(HEADS):
        gate = qg_s[:, aw + hd * HEAD_DIM:aw + (hd + 1) * HEAD_DIM]
        o = (ot_s[hd, 0:HEAD_DIM, :] / ot_s[hd, HEAD_DIM:HEAD_DIM + 1, :]).T
        ys.append((o * (gate * _sigmoid(gate))).astype(BF16))
    y = jnp.concatenate(ys, axis=1)
    out_ref[0] = x + jnp.dot(y, wout_ref[...], preferred_element_type=F32)


def _b_layer(x, qa, k, ka, vt, km, kam, vtm, p, *, blk):
    bsz, t, d = x.shape
    assert t % blk == 0
    nkb = t // blk
    aw = ATTN_WIDTH
    const2 = lambda b, q: (0, 0)
    return pl.pallas_call(
        functools.partial(_b_layer_kernel, blk=blk),
        grid=(bsz, t // blk),
        in_specs=[
            pl.BlockSpec((1, blk, d), lambda b, q: (b, q, 0)),
            pl.BlockSpec((1, blk, V7X_LANES), lambda b, q: (b, q, 0)),
            pl.BlockSpec((1, HEADS, t, HEAD_DIM), lambda b, q: (b, 0, 0, 0)),
            pl.BlockSpec((1, t, V7X_LANES), lambda b, q: (b, 0, 0)),
            pl.BlockSpec((1, HEADS, nkb, V_ROWS, blk), lambda b, q: (b, 0, 0, 0, 0)),
            pl.BlockSpec((1, HEADS, V7X_LANES, HEAD_DIM), lambda b, q: (0, 0, 0, 0)),
            pl.BlockSpec((1, V7X_LANES, V7X_LANES), lambda b, q: (0, 0, 0)),
            pl.BlockSpec((1, HEADS, 1, V_ROWS, V7X_LANES), lambda b, q: (0, 0, 0, 0, 0)),
            pl.BlockSpec((1, d), const2),
            pl.BlockSpec((d, 2 * aw), const2),
            pl.BlockSpec((1, HEAD_DIM), const2),
            pl.BlockSpec((aw, d), const2),
        ],
        out_specs=pl.BlockSpec((1, blk, d), lambda b, q: (b, q, 0)),
        out_shape=jax.ShapeDtypeStruct((bsz, t, d), F32),
        scratch_shapes=[
            pltpu.VMEM((blk, 2 * aw), F32),
            pltpu.VMEM((HEADS, 2 * HEAD_DIM, blk), BF16),
            pltpu.VMEM((HEADS, V_ROWS, blk), F32),
            pltpu.VMEM((HEADS, 1, blk), F32),
        ],
        compiler_params=pltpu.CompilerParams(
            dimension_semantics=("arbitrary", "arbitrary"), vmem_limit_bytes=VMEM_LIMIT),
        name="fox_attention_layer",
    )(x, qa, k, ka, vt, km, kam, vtm, p["gn"], p["win"], p["gq"], p["wout"])


def _a_params(l, a_norm, a_w_in, a_conv_w, a_conv_b, a_w_r, a_b_r, a_w_i, a_b_i, a_lambda, a_w_out):
    wri = jnp.concatenate([a_w_r[l], a_w_i[l]], axis=-1).astype(BF16)
    bri = jnp.concatenate([a_b_r[l].reshape(LRU_BLOCKS, 1, LRU_BLOCK),
                           a_b_i[l].reshape(LRU_BLOCKS, 1, LRU_BLOCK)], axis=-1).astype(F32)
    return dict(gn=a_norm[l].reshape(1, -1).astype(F32), win=a_w_in[l].astype(BF16),
                cw=a_conv_w[l].astype(F32), cb=a_conv_b[l].reshape(1, -1).astype(F32),
                wri=wri, bri=bri, lam=a_lambda[l].reshape(1, -1).astype(F32),
                wout=a_w_out[l].astype(BF16))


def _kv_params(kv_norm, w_kv, b_f, k_norm):
    aw = ATTN_WIDTH
    wf = jnp.repeat(w_kv[:, 2 * aw:2 * aw + HEADS].astype(F32), AUG_LANES, axis=1)
    bfp = jnp.repeat(b_f.astype(F32), AUG_LANES).reshape(1, V7X_LANES)
    return dict(gn=kv_norm.reshape(1, -1).astype(F32), wk=w_kv[:, 0:aw].astype(BF16),
                wv=w_kv[:, aw:2 * aw].astype(BF16), wf=wf.astype(BF16), bf=bfp,
                gk=k_norm.reshape(1, -1).astype(F32))


def kernel(x, meta_tokens, a_norm, a_w_in, a_conv_w, a_conv_b, a_w_r, a_b_r, a_w_i, a_b_i, a_lambda,
           a_w_out, kv_norm, w_kv, b_f, k_norm, b_norm, b_w_in, q_norm, b_w_out):
    n_a = a_norm.shape[0]
    n_b = b_norm.shape[0]
    h = x.astype(F32)
    hm = meta_tokens[None].astype(F32)
    zero_state = jnp.zeros((16, LRU_WIDTH), F32)

    for l in range(n_a):
        p = _a_params(l, a_norm, a_w_in, a_conv_w, a_conv_b, a_w_r, a_b_r, a_w_i, a_b_i, a_lambda,
                      a_w_out)
        hm, st = _a_layer(hm, zero_state, p, chunk=N_META, emit_state=True)
        h = _a_layer(h, st[0], p, chunk=A_CHUNK, emit_state=False)

    pkv = _kv_params(kv_norm, w_kv, b_f, k_norm)
    hm_pad = jnp.pad(hm, ((0, 0), (0, V7X_LANES - N_META), (0, 0)))
    c_zero = jnp.zeros((V7X_SUBLANES, V7X_LANES), F32)
    km, kam, vtm, _, c_meta = _kv_proj(hm_pad, c_zero, pkv, chunk=V7X_LANES, kblock=V7X_LANES,
                                       n_valid=N_META)
    k, ka, vt, qa, _ = _kv_proj(h, c_meta[0], pkv, chunk=KV_CHUNK, kblock=ATT_BLOCK,
                                n_valid=h.shape[1])

    for j in range(n_b):
        p = dict(gn=b_norm[j].reshape(1, -1).astype(F32), win=b_w_in[j].astype(BF16),
                 gq=q_norm[j].reshape(1, -1).astype(F32), wout=b_w_out[j].astype(BF16))
        h = _b_layer(h, qa, k, ka, vt, km, kam, vtm, p, blk=ATT_BLOCK)
    return h
```

```python
import functools

import jax
import jax.numpy as jnp
from jax import lax
from jax.experimental import pallas as pl
from jax.experimental.pallas import tpu as pltpu

F32 = jnp.float32
BF16 = jnp.bfloat16

D_MODEL = 1024
LRU_WIDTH = 1024
LRU_BLOCKS = 4
LRU_BLOCK = LRU_WIDTH // LRU_BLOCKS
CONV_WIDTH = 4
LRU_C = 8.0
HEADS = 8
HEAD_DIM = 128
ATTN_WIDTH = HEADS * HEAD_DIM
N_META = 16
EPS = 1e-6

V7X_SUBLANES = 8
V7X_LANES = 128
AUG_LANES = V7X_LANES // HEADS
MASKED = -1e30
SQRT_FLOOR = 1e-37
LOG2E = 1.4426950408889634
V_ROWS = HEAD_DIM + 16

A_CHUNK = 512
KV_CHUNK = 512
ATT_QUERIES = 512
ATT_BLOCK = 256
VMEM_LIMIT = 56 * 1024 * 1024


def _rms_scale(x):
    return lax.rsqrt(jnp.mean(x * x, axis=-1, keepdims=True) + EPS)


def _sigmoid(x):
    return jax.nn.sigmoid(x)


def _softplus(x):
    return jnp.maximum(x, 0.0) + jnp.log1p(jnp.exp(-jnp.abs(x)))


def _a_layer_kernel(x_ref, st0_ref, gn_ref, win_ref, cw_ref, cb_ref, wri_ref, bri_ref, lam_ref,
                    wout_ref, out_ref, *rest, chunk, emit_state):
    if emit_state:
        st_out_ref, ug_s, ubuf, hcar = rest
    else:
        ug_s, ubuf, hcar = rest
    c = pl.program_id(1)
    w = LRU_WIDTH

    @pl.when(c == 0)
    def _():
        hcar[...] = st0_ref[0:8, :]
        ubuf[...] = st0_ref[8:16, :]

    x = x_ref[0]
    xn = (x * _rms_scale(x) * gn_ref[...]).astype(BF16)
    ug_s[...] = jnp.dot(xn, win_ref[...], preferred_element_type=F32)

    u = ug_s[:, 0:w]
    tail = ubuf[...]
    row8 = lax.broadcasted_iota(jnp.int32, (V7X_SUBLANES, w), 0)

    def delayed(k):
        r = pltpu.roll(u, k, 0)
        top = jnp.where(row8 >= k, r[0:V7X_SUBLANES, :], pltpu.roll(tail, k, 0))
        return jnp.concatenate([top, r[V7X_SUBLANES:, :]], axis=0)

    uc = (cb_ref[...] + cw_ref[0:1, :] * delayed(3) + cw_ref[1:2, :] * delayed(2)
          + cw_ref[2:3, :] * delayed(1) + cw_ref[3:4, :] * u)
    ubuf[...] = u[chunk - V7X_SUBLANES:chunk, :]

    ucb = uc.astype(BF16)
    neg_c_sp = (-LRU_C * LOG2E) * _softplus(-lam_ref[...])
    tiles = chunk // V7X_SUBLANES
    row = lax.broadcasted_iota(jnp.int32, (1, V7X_SUBLANES, LRU_BLOCK), 1)
    ys = []
    for n in range(LRU_BLOCKS):
        lo, hi = n * LRU_BLOCK, (n + 1) * LRU_BLOCK
        ri = jnp.dot(ucb[:, lo:hi], wri_ref[n], preferred_element_type=F32) + bri_ref[n]
        r = _sigmoid(ri[:, 0:LRU_BLOCK])
        i = _sigmoid(ri[:, LRU_BLOCK:2 * LRU_BLOCK])
        a = jnp.exp2(r * neg_c_sp[:, lo:hi])
        om = 1.0 - a * a
        b = om * lax.rsqrt(jnp.maximum(om, SQRT_FLOOR)) * i * uc[:, lo:hi]
        a = a.reshape(tiles, V7X_SUBLANES, LRU_BLOCK)
        b = b.reshape(tiles, V7X_SUBLANES, LRU_BLOCK)
        a_in = jnp.where(row == 0, a, 0.0)
        a1 = jnp.where(row == 0, 0.0, a)
        a2 = a1 * pltpu.roll(a1, 1, 1)
        a4 = a2 * pltpu.roll(a2, 2, 1)
        carry = hcar[0:1, lo:hi]
        hs = []
        for j in range(tiles):
            h = b[j] + a_in[j] * carry
            h = a1[j] * pltpu.roll(h, 1, 0) + h
            h = a2[j] * pltpu.roll(h, 2, 0) + h
            h = a4[j] * pltpu.roll(h, 4, 0) + h
            hs.append(h)
            carry = h[V7X_SUBLANES - 1:V7X_SUBLANES, :]
        hcar[:, lo:hi] = jnp.broadcast_to(carry, (V7X_SUBLANES, LRU_BLOCK))
        gate = ug_s[:, w + lo:w + hi]
        ys.append((jnp.concatenate(hs, axis=0) * (gate * _sigmoid(gate))).astype(BF16))
    y = jnp.concatenate(ys, axis=1)
    out_ref[0] = x + jnp.dot(y, wout_ref[...], preferred_element_type=F32)

    if emit_state:
        @pl.when(c == pl.num_programs(1) - 1)
        def _():
            st_out_ref[0, 0:8, :] = hcar[...]
            st_out_ref[0, 8:16, :] = ubuf[...]


def _a_layer(x, st0, p, *, chunk, emit_state):
    bsz, t, d = x.shape
    assert t % chunk == 0 and chunk % V7X_SUBLANES == 0 and chunk >= 8
    w = LRU_WIDTH
    const2 = lambda b, c: (0, 0)
    const3 = lambda b, c: (0, 0, 0)
    in_specs = [
        pl.BlockSpec((1, chunk, d), lambda b, c: (b, c, 0)),
        pl.BlockSpec((16, w), const2),
        pl.BlockSpec((1, d), const2),
        pl.BlockSpec((d, 2 * w), const2),
        pl.BlockSpec((CONV_WIDTH, w), const2),
        pl.BlockSpec((1, w), const2),
        pl.BlockSpec((LRU_BLOCKS, LRU_BLOCK, 2 * LRU_BLOCK), const3),
        pl.BlockSpec((LRU_BLOCKS, 1, 2 * LRU_BLOCK), const3),
        pl.BlockSpec((1, w), const2),
        pl.BlockSpec((w, d), const2),
    ]
    out_shape = [jax.ShapeDtypeStruct((bsz, t, d), F32)]
    out_specs = [pl.BlockSpec((1, chunk, d), lambda b, c: (b, c, 0))]
    if emit_state:
        out_shape.append(jax.ShapeDtypeStruct((bsz, 16, w), F32))
        out_specs.append(pl.BlockSpec((1, 16, w), lambda b, c: (b, 0, 0)))
    scratch = [
        pltpu.VMEM((chunk, 2 * w), F32),
        pltpu.VMEM((V7X_SUBLANES, w), F32),
        pltpu.VMEM((V7X_SUBLANES, w), F32),
    ]
    res = pl.pallas_call(
        functools.partial(_a_layer_kernel, chunk=chunk, emit_state=emit_state),
        grid=(bsz, t // chunk),
        in_specs=in_specs,
        out_specs=out_specs,
        out_shape=out_shape,
        scratch_shapes=scratch,
        compiler_params=pltpu.CompilerParams(
            dimension_semantics=("arbitrary", "arbitrary"), vmem_limit_bytes=VMEM_LIMIT),
        name="rglru_layer_state" if emit_state else "rglru_layer",
    )(x, st0, p["gn"], p["win"], p["cw"], p["cb"], p["wri"], p["bri"], p["lam"], p["wout"])
    return res if emit_state else res[0]


def _kv_kernel(h_ref, c0_ref, gn_ref, wk_ref, wv_ref, wf_ref, bf_ref, gk_ref,
               k_ref, ka_ref, vt_ref, qa_ref, cfin_ref, ccar, *, chunk, kblock, n_valid):
    c = pl.program_id(1)

    @pl.when(c == 0)
    def _():
        ccar[...] = c0_ref[...]

    x = h_ref[0]
    xn = (x * _rms_scale(x) * gn_ref[...]).astype(BF16)

    k = jnp.dot(xn, wk_ref[...], preferred_element_type=F32)
    for hd in range(HEADS):
        kh = k[:, hd * HEAD_DIM:(hd + 1) * HEAD_DIM]
        k_ref[0, hd] = (kh * _rms_scale(kh) * gk_ref[...]).astype(BF16)

    v = jnp.dot(xn, wv_ref[...], preferred_element_type=F32)
    tail_row = lax.broadcasted_iota(jnp.int32, (V_ROWS - HEAD_DIM, kblock), 0)
    ones_rows = jnp.where(tail_row == 0, 1.0, 0.0).astype(BF16)
    for hd in range(HEADS):
        vt = v[:, hd * HEAD_DIM:(hd + 1) * HEAD_DIM].T
        for kb in range(chunk // kblock):
            vt_ref[0, hd, kb, 0:HEAD_DIM, :] = vt[:, kb * kblock:(kb + 1) * kblock].astype(BF16)
            vt_ref[0, hd, kb, HEAD_DIM:V_ROWS, :] = ones_rows

    f = jnp.dot(xn, wf_ref[...], preferred_element_type=F32) + bf_ref[...]
    cum = jnp.minimum(f, 0.0) - jnp.log1p(jnp.exp(-jnp.abs(f)))
    row = lax.broadcasted_iota(jnp.int32, (chunk, V7X_LANES), 0)
    k_step = 1
    while k_step < chunk:
        cum = cum + jnp.where(row >= k_step, pltpu.roll(cum, k_step, 0), 0.0)
        k_step *= 2
    cum = cum + ccar[0:1, :]
    last = n_valid - 1 if n_valid < chunk else chunk - 1
    ccar[...] = jnp.broadcast_to(cum[last:last + 1, :], (V7X_SUBLANES, V7X_LANES))

    cum2 = cum * LOG2E
    hi = cum2.astype(BF16).astype(F32)
    mid = (cum2 - hi).astype(BF16).astype(F32)
    lo = (cum2 - hi - mid).astype(BF16).astype(F32)
    sub = lax.broadcasted_iota(jnp.int32, (chunk, V7X_LANES), 1) & (AUG_LANES - 1)
    one = jnp.ones_like(cum)
    zero = jnp.zeros_like(cum)
    qa = jnp.where(sub < 3, one, jnp.where(sub == 3, hi, jnp.where(sub == 4, mid,
                                                                  jnp.where(sub == 5, lo, zero))))
    ka = jnp.where(sub == 0, -hi, jnp.where(sub == 1, -mid, jnp.where(sub == 2, -lo,
                                                                      jnp.where(sub < 6, one, zero))))
    if n_valid < chunk:
        ka = jnp.where((row >= n_valid) & (sub == 0), MASKED, ka)
    qa_ref[0] = qa.astype(BF16)
    ka_ref[0] = ka.astype(BF16)

    @pl.when(c == pl.num_programs(1) - 1)
    def _():
        cfin_ref[0] = ccar[...]


def _kv_proj(h, c0, p, *, chunk, kblock, n_valid):
    bsz, t, d = h.shape
    assert t % chunk == 0 and chunk % kblock == 0
    assert n_valid == t or t == chunk
    nkb = t // kblock
    const2 = lambda b, c: (0, 0)
    cpk = chunk // kblock
    outs = pl.pallas_call(
        functools.partial(_kv_kernel, chunk=chunk, kblock=kblock, n_valid=n_valid),
        grid=(bsz, t // chunk),
        in_specs=[
            pl.BlockSpec((1, chunk, d), lambda b, c: (b, c, 0)),
            pl.BlockSpec((V7X_SUBLANES, V7X_LANES), const2),
            pl.BlockSpec((1, d), const2),
            pl.BlockSpec((d, ATTN_WIDTH), const2),
            pl.BlockSpec((d, ATTN_WIDTH), const2),
            pl.BlockSpec((d, V7X_LANES), const2),
            pl.BlockSpec((1, V7X_LANES), const2),
            pl.BlockSpec((1, HEAD_DIM), const2),
        ],
        out_specs=[
            pl.BlockSpec((1, HEADS, chunk, HEAD_DIM), lambda b, c: (b, 0, c, 0)),
            pl.BlockSpec((1, chunk, V7X_LANES), lambda b, c: (b, c, 0)),
            pl.BlockSpec((1, HEADS, cpk, V_ROWS, kblock), lambda b, c: (b, 0, c, 0, 0)),
            pl.BlockSpec((1, chunk, V7X_LANES), lambda b, c: (b, c, 0)),
            pl.BlockSpec((1, V7X_SUBLANES, V7X_LANES), lambda b, c: (b, 0, 0)),
        ],
        out_shape=[
            jax.ShapeDtypeStruct((bsz, HEADS, t, HEAD_DIM), BF16),
            jax.ShapeDtypeStruct((bsz, t, V7X_LANES), BF16),
            jax.ShapeDtypeStruct((bsz, HEADS, nkb, V_ROWS, kblock), BF16),
            jax.ShapeDtypeStruct((bsz, t, V7X_LANES), BF16),
            jax.ShapeDtypeStruct((bsz, V7X_SUBLANES, V7X_LANES), F32),
        ],
        scratch_shapes=[pltpu.VMEM((V7X_SUBLANES, V7X_LANES), F32)],
        compiler_params=pltpu.CompilerParams(
            dimension_semantics=("arbitrary", "arbitrary"), vmem_limit_bytes=VMEM_LIMIT),
        name="shared_kv" if n_valid == t else "shared_kv_meta",
    )(h, c0, p["gn"], p["wk"], p["wv"], p["wf"], p["bf"], p["gk"])
    return outs


def _b_layer_kernel(x_ref, qa_ref, k_ref, ka_ref, vt_ref, km_ref, kam_ref, vtm_ref,
                    gn_ref, win_ref, gq_ref, wout_ref, out_ref,
                    qg_s, qt_s, ot_s, m_s, *, bq, bk):
    qi = pl.program_id(1)
    aw = ATTN_WIDTH
    scale = HEAD_DIM ** -0.5 * LOG2E
    sub_blocks = bq // bk

    x = x_ref[0]
    xn = (x * _rms_scale(x) * gn_ref[...]).astype(BF16)
    qg_s[...] = jnp.dot(xn, win_ref[...], preferred_element_type=F32)

    qa = qa_ref[0].astype(F32)
    lane_head = lax.shift_right_logical(
        lax.broadcasted_iota(jnp.int32, (bq, V7X_LANES), 1), AUG_LANES.bit_length() - 1)
    for hd in range(HEADS):
        qh = qg_s[:, hd * HEAD_DIM:(hd + 1) * HEAD_DIM]
        qn = qh * _rms_scale(qh) * (gq_ref[...] * scale)
        qt_s[hd, 0:HEAD_DIM, :] = qn.T.astype(BF16)
        qt_s[hd, HEAD_DIM:2 * HEAD_DIM, :] = jnp.where(lane_head == hd, qa, 0.0).T.astype(BF16)

    def scores(hd, kj, q0):
        r0 = pl.multiple_of(kj * bk, bk)
        k_aug = jnp.concatenate([k_ref[0, hd, pl.ds(r0, bk), :], ka_ref[0, pl.ds(r0, bk), :]],
                                axis=1)
        return jnp.dot(k_aug, qt_s[hd, :, q0:bq], preferred_element_type=F32)

    def online_update(hd, s, vt_blk, q0):
        m_prev = m_s[hd, :, q0:bq]
        m_next = jnp.maximum(m_prev, jnp.max(s, axis=0, keepdims=True))
        alpha = jnp.exp2(m_prev - m_next)
        p = jnp.exp2(s - m_next).astype(BF16)
        ot_s[hd, :, q0:bq] = alpha * ot_s[hd, :, q0:bq] + jnp.dot(vt_blk, p,
                                                                  preferred_element_type=F32)
        m_s[hd, :, q0:bq] = m_next


    kj = qi * sub_blocks
    r0 = pl.multiple_of(kj * bk, bk)
    ka_blk = jnp.concatenate([kam_ref[0], ka_ref[0, pl.ds(r0, bk), :]], axis=0)
    n_meta = kam_ref.shape[1]
    key_row = lax.broadcasted_iota(jnp.int32, (n_meta + bk, bq), 0) - n_meta
    query_col = lax.broadcasted_iota(jnp.int32, (n_meta + bk, bq), 1)
    causal = key_row <= query_col
    ss = []
    for hd in range(HEADS):
        k_aug = jnp.concatenate(
            [jnp.concatenate([km_ref[0, hd], k_ref[0, hd, pl.ds(r0, bk), :]], axis=0), ka_blk], axis=1)
        ss.append(jnp.dot(k_aug, qt_s[hd], preferred_element_type=F32))
    for hd in range(HEADS):
        s = jnp.where(causal, ss[hd], MASKED)
        m0 = jnp.max(s, axis=0, keepdims=True)
        m_s[hd] = m0
        vt_blk = jnp.concatenate([vtm_ref[0, hd, 0], vt_ref[0, hd, kj]], axis=1)
        ot_s[hd] = jnp.dot(vt_blk, jnp.exp2(s - m0).astype(BF16), preferred_element_type=F32)

    for d in range(1, sub_blocks):
        q0 = d * bk
        kj = qi * sub_blocks + d
        key_row = lax.broadcasted_iota(jnp.int32, (bk, bq - q0), 0)
        query_col = lax.broadcasted_iota(jnp.int32, (bk, bq - q0), 1)
        causal = key_row <= query_col
        ss = [scores(hd, kj, q0) for hd in range(HEADS)]
        for hd in range(HEADS):
            online_update(hd, jnp.where(causal, ss[hd], MASKED), vt_ref[0, hd, kj], q0)

    def kv_body(kj, carry):
        ss = [scores(hd, kj, 0) for hd in range(HEADS)]
        for hd in range(HEADS):
            online_update(hd, ss[hd], vt_ref[0, hd, kj], 0)
        return carry

    lax.fori_loop(0, qi * sub_blocks, kv_body, 0)

    ys = []
    for hd in range(HEADS):
        gate = qg_s[:, aw + hd * HEAD_DIM:aw + (hd + 1) * HEAD_DIM]
        o = (ot_s[hd, 0:HEAD_DIM, :] / ot_s[hd, HEAD_DIM:HEAD_DIM + 1, :]).T
        ys.append((o * (gate * _sigmoid(gate))).astype(BF16))
    y = jnp.concatenate(ys, axis=1)
    out_ref[0] = x + jnp.dot(y, wout_ref[...], preferred_element_type=F32)


def _b_layer(x, qa, k, ka, vt, km, kam, vtm, p, *, bq, bk):
    bsz, t, d = x.shape
    assert t % bq == 0 and bq % bk == 0
    nkb = t // bk
    aw = ATTN_WIDTH
    const2 = lambda b, q: (0, 0)
    resident = dict(pipeline_mode=pl.Buffered(1))
    return pl.pallas_call(
        functools.partial(_b_layer_kernel, bq=bq, bk=bk),
        grid=(bsz, t // bq),
        in_specs=[
            pl.BlockSpec((1, bq, d), lambda b, q: (b, q, 0)),
            pl.BlockSpec((1, bq, V7X_LANES), lambda b, q: (b, q, 0)),
            pl.BlockSpec((1, HEADS, t, HEAD_DIM), lambda b, q: (b, 0, 0, 0)),
            pl.BlockSpec((1, t, V7X_LANES), lambda b, q: (b, 0, 0)),
            pl.BlockSpec((1, HEADS, nkb, V_ROWS, bk), lambda b, q: (b, 0, 0, 0, 0)),
            pl.BlockSpec((1, HEADS, V7X_LANES, HEAD_DIM), lambda b, q: (0, 0, 0, 0)),
            pl.BlockSpec((1, V7X_LANES, V7X_LANES), lambda b, q: (0, 0, 0)),
            pl.BlockSpec((1, HEADS, 1, V_ROWS, V7X_LANES), lambda b, q: (0, 0, 0, 0, 0)),
            pl.BlockSpec((1, d), const2),
            pl.BlockSpec((d, 2 * aw), const2, **resident),
            pl.BlockSpec((1, HEAD_DIM), const2),
            pl.BlockSpec((aw, d), const2, **resident),
        ],
        out_specs=pl.BlockSpec((1, bq, d), lambda b, q: (b, q, 0)),
        out_shape=jax.ShapeDtypeStruct((bsz, t, d), F32),
        scratch_shapes=[
            pltpu.VMEM((bq, 2 * aw), F32),
            pltpu.VMEM((HEADS, 2 * HEAD_DIM, bq), BF16),
            pltpu.VMEM((HEADS, V_ROWS, bq), F32),
            pltpu.VMEM((HEADS, 1, bq), F32),
        ],
        compiler_params=pltpu.CompilerParams(
            dimension_semantics=("arbitrary", "arbitrary"), vmem_limit_bytes=VMEM_LIMIT),
        name="fox_attention_layer",
    )(x, qa, k, ka, vt, km, kam, vtm, p["gn"], p["win"], p["gq"], p["wout"])


def _a_params(l, a_norm, a_w_in, a_conv_w, a_conv_b, a_w_r, a_b_r, a_w_i, a_b_i, a_lambda, a_w_out):
    wri = jnp.concatenate([a_w_r[l], a_w_i[l]], axis=-1).astype(BF16)
    bri = jnp.concatenate([a_b_r[l].reshape(LRU_BLOCKS, 1, LRU_BLOCK),
                           a_b_i[l].reshape(LRU_BLOCKS, 1, LRU_BLOCK)], axis=-1).astype(F32)
    return dict(gn=a_norm[l].reshape(1, -1).astype(F32), win=a_w_in[l].astype(BF16),
                cw=a_conv_w[l].astype(F32), cb=a_conv_b[l].reshape(1, -1).astype(F32),
                wri=wri, bri=bri, lam=a_lambda[l].reshape(1, -1).astype(F32),
                wout=a_w_out[l].astype(BF16))


def _kv_params(kv_norm, w_kv, b_f, k_norm):
    aw = ATTN_WIDTH
    wf = jnp.repeat(w_kv[:, 2 * aw:2 * aw + HEADS].astype(F32), AUG_LANES, axis=1)
    bfp = jnp.repeat(b_f.astype(F32), AUG_LANES).reshape(1, V7X_LANES)
    return dict(gn=kv_norm.reshape(1, -1).astype(F32), wk=w_kv[:, 0:aw].astype(BF16),
                wv=w_kv[:, aw:2 * aw].astype(BF16), wf=wf.astype(BF16), bf=bfp,
                gk=k_norm.reshape(1, -1).astype(F32))


def kernel(x, meta_tokens, a_norm, a_w_in, a_conv_w, a_conv_b, a_w_r, a_b_r, a_w_i, a_b_i, a_lambda,
           a_w_out, kv_norm, w_kv, b_f, k_norm, b_norm, b_w_in, q_norm, b_w_out):
    n_a = a_norm.shape[0]
    n_b = b_norm.shape[0]
    h = x.astype(F32)
    hm = meta_tokens[None].astype(F32)
    zero_state = jnp.zeros((16, LRU_WIDTH), F32)

    for l in range(n_a):
        p = _a_params(l, a_norm, a_w_in, a_conv_w, a_conv_b, a_w_r, a_b_r, a_w_i, a_b_i, a_lambda,
                      a_w_out)
        hm, st = _a_layer(hm, zero_state, p, chunk=N_META, emit_state=True)
        h = _a_layer(h, st[0], p, chunk=A_CHUNK, emit_state=False)

    pkv = _kv_params(kv_norm, w_kv, b_f, k_norm)
    hm_pad = jnp.pad(hm, ((0, 0), (0, V7X_LANES - N_META), (0, 0)))
    c_zero = jnp.zeros((V7X_SUBLANES, V7X_LANES), F32)
    km, kam, vtm, _, c_meta = _kv_proj(hm_pad, c_zero, pkv, chunk=V7X_LANES, kblock=V7X_LANES,
                                       n_valid=N_META)
    k, ka, vt, qa, _ = _kv_proj(h, c_meta[0], pkv, chunk=KV_CHUNK, kblock=ATT_BLOCK,
                                n_valid=h.shape[1])

    for j in range(n_b):
        p = dict(gn=b_norm[j].reshape(1, -1).astype(F32), win=b_w_in[j].astype(BF16),
                 gq=q_norm[j].reshape(1, -1).astype(F32), wout=b_w_out[j].astype(BF16))
        h = _b_layer(h, qa, k, ka, vt, km, kam, vtm, p, bq=ATT_QUERIES, bk=ATT_BLOCK)
    return h
```

```python
import functools

import jax
import jax.numpy as jnp
from jax import lax
from jax.experimental import pallas as pl
from jax.experimental.pallas import tpu as pltpu

F32 = jnp.float32
BF16 = jnp.bfloat16

D_MODEL = 1024
LRU_WIDTH = 1024
LRU_BLOCKS = 4
LRU_BLOCK = LRU_WIDTH // LRU_BLOCKS
CONV_WIDTH = 4
LRU_C = 8.0
HEADS = 8
HEAD_DIM = 128
ATTN_WIDTH = HEADS * HEAD_DIM
N_META = 16
EPS = 1e-6

V7X_SUBLANES = 8
V7X_LANES = 128
AUG_LANES = V7X_LANES // HEADS
MASKED = -1e30
SQRT_FLOOR = 1e-37
LOG2E = 1.4426950408889634
V_ROWS = HEAD_DIM + 16

A_CHUNK = 512
A_GROUP = 256
A_SEG = A_GROUP // V7X_SUBLANES
KV_CHUNK = 512
ATT_QUERIES = 512
ATT_BLOCK = 256
VMEM_LIMIT = 56 * 1024 * 1024


def _rms_scale(x):
    return lax.rsqrt(jnp.mean(x * x, axis=-1, keepdims=True) + EPS)


def _sigmoid(x):
    return jax.nn.sigmoid(x)


def _softplus(x):
    return jnp.maximum(x, 0.0) + jnp.log1p(jnp.exp(-jnp.abs(x)))


def _a_layer_kernel(x_ref, st0_ref, gn_ref, win_ref, cw_ref, cb_ref, wri_ref, bri_ref, lam_ref,
                    wout_ref, out_ref, *rest, chunk, emit_state):
    if emit_state:
        st_out_ref, ug_s, ubuf, hcar = rest
    else:
        ug_s, ubuf, hcar = rest
    c = pl.program_id(1)
    w = LRU_WIDTH

    @pl.when(c == 0)
    def _():
        hcar[...] = st0_ref[0:8, :]
        ubuf[...] = st0_ref[8:16, :]

    x = x_ref[0]
    xn = (x * _rms_scale(x) * gn_ref[...]).astype(BF16)
    ug_s[...] = jnp.dot(xn, win_ref[...], preferred_element_type=F32)

    u = ug_s[:, 0:w]
    tail = ubuf[...]
    row8 = lax.broadcasted_iota(jnp.int32, (V7X_SUBLANES, w), 0)

    def delayed(k):
        r = pltpu.roll(u, k, 0)
        top = jnp.where(row8 >= k, r[0:V7X_SUBLANES, :], pltpu.roll(tail, k, 0))
        return jnp.concatenate([top, r[V7X_SUBLANES:, :]], axis=0)

    uc = (cb_ref[...] + cw_ref[0:1, :] * delayed(3) + cw_ref[1:2, :] * delayed(2)
          + cw_ref[2:3, :] * delayed(1) + cw_ref[3:4, :] * u)
    ubuf[...] = u[chunk - V7X_SUBLANES:chunk, :]

    ucb = uc.astype(BF16)
    neg_c_sp = (-LRU_C * LOG2E) * _softplus(-lam_ref[...])
    tiles = chunk // V7X_SUBLANES
    row = lax.broadcasted_iota(jnp.int32, (1, V7X_SUBLANES, LRU_BLOCK), 1)
    ys = []
    for n in range(LRU_BLOCKS):
        lo, hi = n * LRU_BLOCK, (n + 1) * LRU_BLOCK
        ri = jnp.dot(ucb[:, lo:hi], wri_ref[n], preferred_element_type=F32) + bri_ref[n]
        r = _sigmoid(ri[:, 0:LRU_BLOCK])
        i = _sigmoid(ri[:, LRU_BLOCK:2 * LRU_BLOCK])
        a = jnp.exp2(r * neg_c_sp[:, lo:hi])
        om = 1.0 - a * a
        b = om * lax.rsqrt(jnp.maximum(om, SQRT_FLOOR)) * i * uc[:, lo:hi]
        a = a.reshape(tiles, V7X_SUBLANES, LRU_BLOCK)
        b = b.reshape(tiles, V7X_SUBLANES, LRU_BLOCK)
        a_in = jnp.where(row == 0, a, 0.0)
        a1 = jnp.where(row == 0, 0.0, a)
        a2 = a1 * pltpu.roll(a1, 1, 1)
        a4 = a2 * pltpu.roll(a2, 2, 1)
        carry = hcar[0:1, lo:hi]
        hs = []
        for j in range(tiles):
            h = b[j] + a_in[j] * carry
            h = a1[j] * pltpu.roll(h, 1, 0) + h
            h = a2[j] * pltpu.roll(h, 2, 0) + h
            h = a4[j] * pltpu.roll(h, 4, 0) + h
            hs.append(h)
            carry = h[V7X_SUBLANES - 1:V7X_SUBLANES, :]
        hcar[:, lo:hi] = jnp.broadcast_to(carry, (V7X_SUBLANES, LRU_BLOCK))
        gate = ug_s[:, w + lo:w + hi]
        ys.append((jnp.concatenate(hs, axis=0) * (gate * _sigmoid(gate))).astype(BF16))
    y = jnp.concatenate(ys, axis=1)
    out_ref[0] = x + jnp.dot(y, wout_ref[...], preferred_element_type=F32)

    if emit_state:
        @pl.when(c == pl.num_programs(1) - 1)
        def _():
            st_out_ref[0, 0:8, :] = hcar[...]
            st_out_ref[0, 8:16, :] = ubuf[...]


def _a_layer_perm_kernel(x_ref, st0_ref, gn_ref, win_ref, cw_ref, cb_ref, wri_ref, bri_ref, lam_ref,
                         wout_ref, perm_ref, permt_ref, out_ref, xs, ug_s, ubuf, hcar,
                         *, chunk, chunks_per_row):
    i_step = pl.program_id(0)

    @pl.when(i_step == 0)
    def _():
        xs[1] = jnp.zeros((chunk, D_MODEL), F32)
        ug_s[1] = jnp.zeros((chunk, 2 * LRU_WIDTH), F32)
        hcar[...] = jnp.zeros_like(hcar)
        ubuf[...] = jnp.zeros_like(ubuf)

    refs = (x_ref, st0_ref, gn_ref, win_ref, cw_ref, cb_ref, wri_ref, bri_ref, lam_ref, wout_ref,
            perm_ref, permt_ref, out_ref, xs, ug_s, ubuf, hcar)
    for parity in (0, 1):
        pl.when(lax.rem(i_step, 2) == parity)(
            functools.partial(_a_layer_perm_step, parity, *refs, chunk=chunk,
                              chunks_per_row=chunks_per_row))


def _a_layer_perm_step(slot_new, x_ref, st0_ref, gn_ref, win_ref, cw_ref, cb_ref, wri_ref, bri_ref,
                       lam_ref, wout_ref, perm_ref, permt_ref, out_ref, xs, ug_s, ubuf, hcar,
                       *, chunk, chunks_per_row):
    i_step = pl.program_id(0)
    w = LRU_WIDTH
    groups = chunk // A_GROUP
    tiles = A_GROUP // V7X_SUBLANES
    slot_old = 1 - slot_new

    row_start = lax.rem(jnp.maximum(i_step - 1, 0), chunks_per_row) == 0
    tail = jnp.where(row_start, st0_ref[8:16, :], ubuf[...])
    h_prev = jnp.where(row_start, st0_ref[0:1, :], hcar[0:1, :])

    x_new = x_ref[0]
    xs[slot_new] = x_new
    xn = (x_new * _rms_scale(x_new) * gn_ref[...]).astype(BF16)
    xp = jnp.concatenate(
        [jnp.dot(perm_ref[...], xn[g * A_GROUP:(g + 1) * A_GROUP, :],
                 preferred_element_type=F32).astype(BF16) for g in range(groups)], axis=0)

    neg_c_sp = (-LRU_C * LOG2E) * _softplus(-lam_ref[...])
    row = lax.broadcasted_iota(jnp.int32, (V7X_SUBLANES, LRU_BLOCK), 0)


    def project(n):
        lo, hi = n * LRU_BLOCK, (n + 1) * LRU_BLOCK
        ug_s[slot_new, :, lo:hi] = jnp.dot(xp, win_ref[:, lo:hi], preferred_element_type=F32)
        ug_s[slot_new, :, w + lo:w + hi] = jnp.dot(xp, win_ref[:, w + lo:w + hi],
                                                   preferred_element_type=F32)

    def conv_and_gates(n):
        lo, hi = n * LRU_BLOCK, (n + 1) * LRU_BLOCK
        prev = [tail[5 + j:6 + j, lo:hi] for j in range(3)]
        ucs = []
        for g in range(groups):
            ug = ug_s[slot_old, g * A_GROUP:(g + 1) * A_GROUP, lo:hi]
            last = [ug[(tiles - 3 + j) * V7X_SUBLANES:(tiles - 2 + j) * V7X_SUBLANES, :]
                    for j in range(3)]
            hist = [jnp.where(row == 0, prev[j], pltpu.roll(last[j], 1, 0)) for j in range(3)]
            u_ext = jnp.concatenate(hist + [ug], axis=0)
            ucs.append(cb_ref[:, lo:hi] + cw_ref[0:1, lo:hi] * u_ext[0:A_GROUP, :]
                       + cw_ref[1:2, lo:hi] * u_ext[8:8 + A_GROUP, :]
                       + cw_ref[2:3, lo:hi] * u_ext[16:16 + A_GROUP, :] + cw_ref[3:4, lo:hi] * ug)
            prev = [t[V7X_SUBLANES - 1:V7X_SUBLANES, :] for t in last]
        ubuf[:, lo:hi] = jnp.where(row == 5, prev[0], jnp.where(row == 6, prev[1], prev[2]))
        uc = jnp.concatenate(ucs, axis=0)
        ri = jnp.dot(uc.astype(BF16), wri_ref[n], preferred_element_type=F32) + bri_ref[n]
        return uc, ri

    def recur_and_project(n, uc, ri):
        lo, hi = n * LRU_BLOCK, (n + 1) * LRU_BLOCK
        r = _sigmoid(ri[:, 0:LRU_BLOCK])
        i = _sigmoid(ri[:, LRU_BLOCK:2 * LRU_BLOCK])
        a = jnp.exp2(r * neg_c_sp[:, lo:hi])
        om = 1.0 - a * a
        b = om * lax.rsqrt(jnp.maximum(om, SQRT_FLOOR)) * i * uc
        a = a.reshape(groups, tiles, V7X_SUBLANES, LRU_BLOCK)
        b = b.reshape(groups, tiles, V7X_SUBLANES, LRU_BLOCK)
        carry = h_prev[:, lo:hi]
        hs = []
        for g in range(groups):
            h0, dec = [b[g, 0]], [a[g, 0]]
            for t in range(1, tiles):
                h0.append(a[g, t] * h0[-1] + b[g, t])
                dec.append(a[g, t] * dec[-1])
            a_end, h_end = dec[-1], h0[-1]
            a1 = jnp.where(row == 0, 0.0, a_end)
            a2 = a1 * pltpu.roll(a1, 1, 0)
            a4 = a2 * pltpu.roll(a2, 2, 0)
            h_end = h_end + jnp.where(row == 0, a_end, 0.0) * carry
            h_end = a1 * pltpu.roll(h_end, 1, 0) + h_end
            h_end = a2 * pltpu.roll(h_end, 2, 0) + h_end
            h_end = a4 * pltpu.roll(h_end, 4, 0) + h_end
            h_in = jnp.where(row == 0, carry, pltpu.roll(h_end, 1, 0))
            hs.extend(h0[t] + dec[t] * h_in for t in range(tiles))
            carry = h_end[V7X_SUBLANES - 1:V7X_SUBLANES, :]
        hcar[:, lo:hi] = jnp.broadcast_to(carry, (V7X_SUBLANES, LRU_BLOCK))
        gate = ug_s[slot_old, :, w + lo:w + hi]
        yp = (jnp.concatenate(hs, axis=0) * (gate * _sigmoid(gate))).astype(BF16)
        y = jnp.concatenate(
            [jnp.dot(permt_ref[...], yp[g * A_GROUP:(g + 1) * A_GROUP, :],
                     preferred_element_type=F32).astype(BF16) for g in range(groups)], axis=0)
        return jnp.dot(y, wout_ref[lo:hi, :], preferred_element_type=F32)

    out = xs[slot_old]
    mid = conv_and_gates(0)
    for n in range(LRU_BLOCKS):
        project(n)
        mid_next = conv_and_gates(n + 1) if n + 1 < LRU_BLOCKS else None
        out = out + recur_and_project(n, *mid)
        mid = mid_next
    out_ref[0] = out


def _a_layer_perm(x, st0, p, *, chunk):
    bsz, t, d = x.shape
    assert t % chunk == 0 and chunk % A_GROUP == 0
    w = LRU_WIDTH
    nc = t // chunk
    steps = bsz * nc
    const2 = lambda i: (0, 0)
    const3 = lambda i: (0, 0, 0)
    resident = dict(pipeline_mode=pl.Buffered(1))

    def chunk_in(i):
        j = jnp.minimum(i, steps - 1)
        return (j // nc, j % nc, 0)

    def chunk_out(i):
        j = jnp.maximum(i - 1, 0)
        return (j // nc, j % nc, 0)

    r = jnp.arange(A_GROUP)
    src = A_SEG * (r % V7X_SUBLANES) + r // V7X_SUBLANES
    perm = (src[:, None] == jnp.arange(A_GROUP)[None, :]).astype(BF16)
    return pl.pallas_call(
        functools.partial(_a_layer_perm_kernel, chunk=chunk, chunks_per_row=nc),
        grid=(steps + 1,),
        in_specs=[
            pl.BlockSpec((1, chunk, d), chunk_in),
            pl.BlockSpec((16, w), const2),
            pl.BlockSpec((1, d), const2),
            pl.BlockSpec((d, 2 * w), const2, **resident),
            pl.BlockSpec((CONV_WIDTH, w), const2),
            pl.BlockSpec((1, w), const2),
            pl.BlockSpec((LRU_BLOCKS, LRU_BLOCK, 2 * LRU_BLOCK), const3, **resident),
            pl.BlockSpec((LRU_BLOCKS, 1, 2 * LRU_BLOCK), const3),
            pl.BlockSpec((1, w), const2),
            pl.BlockSpec((w, d), const2, **resident),
            pl.BlockSpec((A_GROUP, A_GROUP), const2),
            pl.BlockSpec((A_GROUP, A_GROUP), const2),
        ],
        out_specs=pl.BlockSpec((1, chunk, d), chunk_out),
        out_shape=jax.ShapeDtypeStruct((bsz, t, d), F32),
        scratch_shapes=[
            pltpu.VMEM((2, chunk, d), F32),
            pltpu.VMEM((2, chunk, 2 * w), F32),
            pltpu.VMEM((V7X_SUBLANES, w), F32),
            pltpu.VMEM((V7X_SUBLANES, w), F32),
        ],
        compiler_params=pltpu.CompilerParams(
            dimension_semantics=("arbitrary",), vmem_limit_bytes=VMEM_LIMIT),
        name="rglru_layer",
    )(x, st0, p["gn"], p["win"], p["cw"], p["cb"], p["wri"], p["bri"], p["lam"], p["wout"],
      perm, perm.T)


def _a_layer(x, st0, p, *, chunk, emit_state):
    bsz, t, d = x.shape
    assert t % chunk == 0 and chunk % V7X_SUBLANES == 0 and chunk >= 8
    w = LRU_WIDTH
    const2 = lambda b, c: (0, 0)
    const3 = lambda b, c: (0, 0, 0)
    in_specs = [
        pl.BlockSpec((1, chunk, d), lambda b, c: (b, c, 0)),
        pl.BlockSpec((16, w), const2),
        pl.BlockSpec((1, d), const2),
        pl.BlockSpec((d, 2 * w), const2),
        pl.BlockSpec((CONV_WIDTH, w), const2),
        pl.BlockSpec((1, w), const2),
        pl.BlockSpec((LRU_BLOCKS, LRU_BLOCK, 2 * LRU_BLOCK), const3),
        pl.BlockSpec((LRU_BLOCKS, 1, 2 * LRU_BLOCK), const3),
        pl.BlockSpec((1, w), const2),
        pl.BlockSpec((w, d), const2),
    ]
    out_shape = [jax.ShapeDtypeStruct((bsz, t, d), F32)]
    out_specs = [pl.BlockSpec((1, chunk, d), lambda b, c: (b, c, 0))]
    if emit_state:
        out_shape.append(jax.ShapeDtypeStruct((bsz, 16, w), F32))
        out_specs.append(pl.BlockSpec((1, 16, w), lambda b, c: (b, 0, 0)))
    scratch = [
        pltpu.VMEM((chunk, 2 * w), F32),
        pltpu.VMEM((V7X_SUBLANES, w), F32),
        pltpu.VMEM((V7X_SUBLANES, w), F32),
    ]
    res = pl.pallas_call(
        functools.partial(_a_layer_kernel, chunk=chunk, emit_state=emit_state),
        grid=(bsz, t // chunk),
        in_specs=in_specs,
        out_specs=out_specs,
        out_shape=out_shape,
        scratch_shapes=scratch,
        compiler_params=pltpu.CompilerParams(
            dimension_semantics=("arbitrary", "arbitrary"), vmem_limit_bytes=VMEM_LIMIT),
        name="rglru_layer_state" if emit_state else "rglru_layer",
    )(x, st0, p["gn"], p["win"], p["cw"], p["cb"], p["wri"], p["bri"], p["lam"], p["wout"])
    return res if emit_state else res[0]


def _kv_kernel(h_ref, c0_ref, gn_ref, wk_ref, wv_ref, wf_ref, bf_ref, gk_ref,
               k_ref, ka_ref, vt_ref, qa_ref, cfin_ref, ccar, *, chunk, kblock, n_valid):
    c = pl.program_id(1)

    @pl.when(c == 0)
    def _():
        ccar[...] = c0_ref[...]

    x = h_ref[0]
    xn = (x * _rms_scale(x) * gn_ref[...]).astype(BF16)

    k = jnp.dot(xn, wk_ref[...], preferred_element_type=F32)
    for hd in range(HEADS):
        kh = k[:, hd * HEAD_DIM:(hd + 1) * HEAD_DIM]
        k_ref[0, hd] = (kh * _rms_scale(kh) * gk_ref[...]).astype(BF16)

    v = jnp.dot(xn, wv_ref[...], preferred_element_type=F32)
    tail_row = lax.broadcasted_iota(jnp.int32, (V_ROWS - HEAD_DIM, kblock), 0)
    ones_rows = jnp.where(tail_row == 0, 1.0, 0.0).astype(BF16)
    for hd in range(HEADS):
        vt = v[:, hd * HEAD_DIM:(hd + 1) * HEAD_DIM].T
        for kb in range(chunk // kblock):
            vt_ref[0, hd, kb, 0:HEAD_DIM, :] = vt[:, kb * kblock:(kb + 1) * kblock].astype(BF16)
            vt_ref[0, hd, kb, HEAD_DIM:V_ROWS, :] = ones_rows

    f = jnp.dot(xn, wf_ref[...], preferred_element_type=F32) + bf_ref[...]
    cum = jnp.minimum(f, 0.0) - jnp.log1p(jnp.exp(-jnp.abs(f)))
    row = lax.broadcasted_iota(jnp.int32, (chunk, V7X_LANES), 0)
    k_step = 1
    while k_step < chunk:
        cum = cum + jnp.where(row >= k_step, pltpu.roll(cum, k_step, 0), 0.0)
        k_step *= 2
    cum = cum + ccar[0:1, :]
    last = n_valid - 1 if n_valid < chunk else chunk - 1
    ccar[...] = jnp.broadcast_to(cum[last:last + 1, :], (V7X_SUBLANES, V7X_LANES))

    cum2 = cum * LOG2E
    hi = cum2.astype(BF16).astype(F32)
    mid = (cum2 - hi).astype(BF16).astype(F32)
    lo = (cum2 - hi - mid).astype(BF16).astype(F32)
    sub = lax.broadcasted_iota(jnp.int32, (chunk, V7X_LANES), 1) & (AUG_LANES - 1)
    one = jnp.ones_like(cum)
    zero = jnp.zeros_like(cum)
    qa = jnp.where(sub < 3, one, jnp.where(sub == 3, hi, jnp.where(sub == 4, mid,
                                                                  jnp.where(sub == 5, lo, zero))))
    ka = jnp.where(sub == 0, -hi, jnp.where(sub == 1, -mid, jnp.where(sub == 2, -lo,
                                                                      jnp.where(sub < 6, one, zero))))
    if n_valid < chunk:
        ka = jnp.where((row >= n_valid) & (sub == 0), MASKED, ka)
    qa_ref[0] = qa.astype(BF16)
    ka_ref[0] = ka.astype(BF16)

    @pl.when(c == pl.num_programs(1) - 1)
    def _():
        cfin_ref[0] = ccar[...]


def _kv_proj(h, c0, p, *, chunk, kblock, n_valid):
    bsz, t, d = h.shape
    assert t % chunk == 0 and chunk % kblock == 0
    assert n_valid == t or t == chunk
    nkb = t // kblock
    const2 = lambda b, c: (0, 0)
    cpk = chunk // kblock
    outs = pl.pallas_call(
        functools.partial(_kv_kernel, chunk=chunk, kblock=kblock, n_valid=n_valid),
        grid=(bsz, t // chunk),
        in_specs=[
            pl.BlockSpec((1, chunk, d), lambda b, c: (b, c, 0)),
            pl.BlockSpec((V7X_SUBLANES, V7X_LANES), const2),
            pl.BlockSpec((1, d), const2),
            pl.BlockSpec((d, ATTN_WIDTH), const2),
            pl.BlockSpec((d, ATTN_WIDTH), const2),
            pl.BlockSpec((d, V7X_LANES), const2),
            pl.BlockSpec((1, V7X_LANES), const2),
            pl.BlockSpec((1, HEAD_DIM), const2),
        ],
        out_specs=[
            pl.BlockSpec((1, HEADS, chunk, HEAD_DIM), lambda b, c: (b, 0, c, 0)),
            pl.BlockSpec((1, chunk, V7X_LANES), lambda b, c: (b, c, 0)),
            pl.BlockSpec((1, HEADS, cpk, V_ROWS, kblock), lambda b, c: (b, 0, c, 0, 0)),
            pl.BlockSpec((1, chunk, V7X_LANES), lambda b, c: (b, c, 0)),
            pl.BlockSpec((1, V7X_SUBLANES, V7X_LANES), lambda b, c: (b, 0, 0)),
        ],
        out_shape=[
            jax.ShapeDtypeStruct((bsz, HEADS, t, HEAD_DIM), BF16),
            jax.ShapeDtypeStruct((bsz, t, V7X_LANES), BF16),
            jax.ShapeDtypeStruct((bsz, HEADS, nkb, V_ROWS, kblock), BF16),
            jax.ShapeDtypeStruct((bsz, t, V7X_LANES), BF16),
            jax.ShapeDtypeStruct((bsz, V7X_SUBLANES, V7X_LANES), F32),
        ],
        scratch_shapes=[pltpu.VMEM((V7X_SUBLANES, V7X_LANES), F32)],
        compiler_params=pltpu.CompilerParams(
            dimension_semantics=("arbitrary", "arbitrary"), vmem_limit_bytes=VMEM_LIMIT),
        name="shared_kv" if n_valid == t else "shared_kv_meta",
    )(h, c0, p["gn"], p["wk"], p["wv"], p["wf"], p["bf"], p["gk"])
    return outs


def _b_layer_kernel(x_ref, qa_ref, k_ref, ka_ref, vt_ref, km_ref, kam_ref, vtm_ref,
                    gn_ref, win_ref, gq_ref, wout_ref, out_ref,
                    qg_s, qt_s, ot_s, m_s, *, bq, bk):
    qi = pl.program_id(1)
    aw = ATTN_WIDTH
    scale = HEAD_DIM ** -0.5 * LOG2E
    sub_blocks = bq // bk

    x = x_ref[0]
    xn = (x * _rms_scale(x) * gn_ref[...]).astype(BF16)
    qg_s[...] = jnp.dot(xn, win_ref[...], preferred_element_type=F32)

    qa = qa_ref[0].astype(F32)
    lane_head = lax.shift_right_logical(
        lax.broadcasted_iota(jnp.int32, (bq, V7X_LANES), 1), AUG_LANES.bit_length() - 1)
    for hd in range(HEADS):
        qh = qg_s[:, hd * HEAD_DIM:(hd + 1) * HEAD_DIM]
        qn = qh * _rms_scale(qh) * (gq_ref[...] * scale)
        qt_s[hd, 0:HEAD_DIM, :] = qn.T.astype(BF16)
        qt_s[hd, HEAD_DIM:2 * HEAD_DIM, :] = jnp.where(lane_head == hd, qa, 0.0).T.astype(BF16)

    def scores(hd, kj, q0):
        r0 = pl.multiple_of(kj * bk, bk)
        k_aug = jnp.concatenate([k_ref[0, hd, pl.ds(r0, bk), :], ka_ref[0, pl.ds(r0, bk), :]],
                                axis=1)
        return jnp.dot(k_aug, qt_s[hd, :, q0:bq], preferred_element_type=F32)

    def online_update(hd, s, vt_blk, q0):
        m_prev = m_s[hd, :, q0:bq]
        m_next = jnp.maximum(m_prev, jnp.max(s, axis=0, keepdims=True))
        alpha = jnp.exp2(m_prev - m_next)
        p = jnp.exp2(s - m_next).astype(BF16)
        ot_s[hd, :, q0:bq] = alpha * ot_s[hd, :, q0:bq] + jnp.dot(vt_blk, p,
                                                                  preferred_element_type=F32)
        m_s[hd, :, q0:bq] = m_next


    kj = qi * sub_blocks
    r0 = pl.multiple_of(kj * bk, bk)
    ka_blk = jnp.concatenate([kam_ref[0], ka_ref[0, pl.ds(r0, bk), :]], axis=0)
    n_meta = kam_ref.shape[1]
    key_row = lax.broadcasted_iota(jnp.int32, (n_meta + bk, bq), 0) - n_meta
    query_col = lax.broadcasted_iota(jnp.int32, (n_meta + bk, bq), 1)
    causal = key_row <= query_col
    ss = []
    for hd in range(HEADS):
        k_aug = jnp.concatenate(
            [jnp.concatenate([km_ref[0, hd], k_ref[0, hd, pl.ds(r0, bk), :]], axis=0), ka_blk], axis=1)
        ss.append(jnp.dot(k_aug, qt_s[hd], preferred_element_type=F32))
    for hd in range(HEADS):
        s = jnp.where(causal, ss[hd], MASKED)
        m0 = jnp.max(s, axis=0, keepdims=True)
        m_s[hd] = m0
        vt_blk = jnp.concatenate([vtm_ref[0, hd, 0], vt_ref[0, hd, kj]], axis=1)
        ot_s[hd] = jnp.dot(vt_blk, jnp.exp2(s - m0).astype(BF16), preferred_element_type=F32)

    for d in range(1, sub_blocks):
        q0 = d * bk
        kj = qi * sub_blocks + d
        key_row = lax.broadcasted_iota(jnp.int32, (bk, bq - q0), 0)
        query_col = lax.broadcasted_iota(jnp.int32, (bk, bq - q0), 1)
        causal = key_row <= query_col
        ss = [scores(hd, kj, q0) for hd in range(HEADS)]
        for hd in range(HEADS):
            online_update(hd, jnp.where(causal, ss[hd], MASKED), vt_ref[0, hd, kj], q0)

    def kv_body(kj, carry):
        ss = [scores(hd, kj, 0) for hd in range(HEADS)]
        for hd in range(HEADS):
            online_update(hd, ss[hd], vt_ref[0, hd, kj], 0)
        return carry

    lax.fori_loop(0, qi * sub_blocks, kv_body, 0)

    ys = []
    for hd in range(HEADS):
        gate = qg_s[:, aw + hd * HEAD_DIM:aw + (hd + 1) * HEAD_DIM]
        o = (ot_s[hd, 0:HEAD_DIM, :] / ot_s[hd, HEAD_DIM:HEAD_DIM + 1, :]).T
        ys.append((o * (gate * _sigmoid(gate))).astype(BF16))
    y = jnp.concatenate(ys, axis=1)
    out_ref[0] = x + jnp.dot(y, wout_ref[...], preferred_element_type=F32)


def _b_layer(x, qa, k, ka, vt, km, kam, vtm, p, *, bq, bk):
    bsz, t, d = x.shape
    assert t % bq == 0 and bq % bk == 0
    nkb = t // bk
    aw = ATTN_WIDTH
    const2 = lambda b, q: (0, 0)
    resident = dict(pipeline_mode=pl.Buffered(1))
    return pl.pallas_call(
        functools.partial(_b_layer_kernel, bq=bq, bk=bk),
        grid=(bsz, t // bq),
        in_specs=[
            pl.BlockSpec((1, bq, d), lambda b, q: (b, q, 0)),
            pl.BlockSpec((1, bq, V7X_LANES), lambda b, q: (b, q, 0)),
            pl.BlockSpec((1, HEADS, t, HEAD_DIM), lambda b, q: (b, 0, 0, 0)),
            pl.BlockSpec((1, t, V7X_LANES), lambda b, q: (b, 0, 0)),
            pl.BlockSpec((1, HEADS, nkb, V_ROWS, bk), lambda b, q: (b, 0, 0, 0, 0)),
            pl.BlockSpec((1, HEADS, V7X_LANES, HEAD_DIM), lambda b, q: (0, 0, 0, 0)),
            pl.BlockSpec((1, V7X_LANES, V7X_LANES), lambda b, q: (0, 0, 0)),
            pl.BlockSpec((1, HEADS, 1, V_ROWS, V7X_LANES), lambda b, q: (0, 0, 0, 0, 0)),
            pl.BlockSpec((1, d), const2),
            pl.BlockSpec((d, 2 * aw), const2, **resident),
            pl.BlockSpec((1, HEAD_DIM), const2),
            pl.BlockSpec((aw, d), const2, **resident),
        ],
        out_specs=pl.BlockSpec((1, bq, d), lambda b, q: (b, q, 0)),
        out_shape=jax.ShapeDtypeStruct((bsz, t, d), F32),
        scratch_shapes=[
            pltpu.VMEM((bq, 2 * aw), F32),
            pltpu.VMEM((HEADS, 2 * HEAD_DIM, bq), BF16),
            pltpu.VMEM((HEADS, V_ROWS, bq), F32),
            pltpu.VMEM((HEADS, 1, bq), F32),
        ],
        compiler_params=pltpu.CompilerParams(
            dimension_semantics=("arbitrary", "arbitrary"), vmem_limit_bytes=VMEM_LIMIT),
        name="fox_attention_layer",
    )(x, qa, k, ka, vt, km, kam, vtm, p["gn"], p["win"], p["gq"], p["wout"])


def _a_params(l, a_norm, a_w_in, a_conv_w, a_conv_b, a_w_r, a_b_r, a_w_i, a_b_i, a_lambda, a_w_out):
    wri = jnp.concatenate([a_w_r[l], a_w_i[l]], axis=-1).astype(BF16)
    bri = jnp.concatenate([a_b_r[l].reshape(LRU_BLOCKS, 1, LRU_BLOCK),
                           a_b_i[l].reshape(LRU_BLOCKS, 1, LRU_BLOCK)], axis=-1).astype(F32)
    return dict(gn=a_norm[l].reshape(1, -1).astype(F32), win=a_w_in[l].astype(BF16),
                cw=a_conv_w[l].astype(F32), cb=a_conv_b[l].reshape(1, -1).astype(F32),
                wri=wri, bri=bri, lam=a_lambda[l].reshape(1, -1).astype(F32),
                wout=a_w_out[l].astype(BF16))


def _kv_params(kv_norm, w_kv, b_f, k_norm):
    aw = ATTN_WIDTH
    wf = jnp.repeat(w_kv[:, 2 * aw:2 * aw + HEADS].astype(F32), AUG_LANES, axis=1)
    bfp = jnp.repeat(b_f.astype(F32), AUG_LANES).reshape(1, V7X_LANES)
    return dict(gn=kv_norm.reshape(1, -1).astype(F32), wk=w_kv[:, 0:aw].astype(BF16),
                wv=w_kv[:, aw:2 * aw].astype(BF16), wf=wf.astype(BF16), bf=bfp,
                gk=k_norm.reshape(1, -1).astype(F32))


def kernel(x, meta_tokens, a_norm, a_w_in, a_conv_w, a_conv_b, a_w_r, a_b_r, a_w_i, a_b_i, a_lambda,
           a_w_out, kv_norm, w_kv, b_f, k_norm, b_norm, b_w_in, q_norm, b_w_out):
    n_a = a_norm.shape[0]
    n_b = b_norm.shape[0]
    h = x.astype(F32)
    hm = meta_tokens[None].astype(F32)
    zero_state = jnp.zeros((16, LRU_WIDTH), F32)

    for l in range(n_a):
        p = _a_params(l, a_norm, a_w_in, a_conv_w, a_conv_b, a_w_r, a_b_r, a_w_i, a_b_i, a_lambda,
                      a_w_out)
        hm, st = _a_layer(hm, zero_state, p, chunk=N_META, emit_state=True)
        h = _a_layer_perm(h, st[0], p, chunk=A_CHUNK)

    pkv = _kv_params(kv_norm, w_kv, b_f, k_norm)
    hm_pad = jnp.pad(hm, ((0, 0), (0, V7X_LANES - N_META), (0, 0)))
    c_zero = jnp.zeros((V7X_SUBLANES, V7X_LANES), F32)
    km, kam, vtm, _, c_meta = _kv_proj(hm_pad, c_zero, pkv, chunk=V7X_LANES, kblock=V7X_LANES,
                                       n_valid=N_META)
    k, ka, vt, qa, _ = _kv_proj(h, c_meta[0], pkv, chunk=KV_CHUNK, kblock=ATT_BLOCK,
                                n_valid=h.shape[1])

    for j in range(n_b):
        p = dict(gn=b_norm[j].reshape(1, -1).astype(F32), win=b_w_in[j].astype(BF16),
                 gq=q_norm[j].reshape(1, -1).astype(F32), wout=b_w_out[j].astype(BF16))
        h = _b_layer(h, qa, k, ka, vt, km, kam, vtm, p, bq=ATT_QUERIES, bk=ATT_BLOCK)
    return h
```

```python
import functools

import jax
import jax.numpy as jnp
from jax import lax
from jax.experimental import pallas as pl
from jax.experimental.pallas import tpu as pltpu

F32 = jnp.float32
BF16 = jnp.bfloat16

D_MODEL = 1024
LRU_WIDTH = 1024
LRU_BLOCKS = 4
LRU_BLOCK = LRU_WIDTH // LRU_BLOCKS
CONV_WIDTH = 4
LRU_C = 8.0
HEADS = 8
HEAD_DIM = 128
ATTN_WIDTH = HEADS * HEAD_DIM
N_META = 16
EPS = 1e-6

V7X_SUBLANES = 8
V7X_LANES = 128
AUG_LANES = V7X_LANES // HEADS
MASKED = -1e30
SQRT_FLOOR = 1e-37
LOG2E = 1.4426950408889634
V_ROWS = HEAD_DIM + 16

A_CHUNK = 512
A_GROUP = 256
A_SEG = A_GROUP // V7X_SUBLANES
KV_CHUNK = 512
ATT_QUERIES = 512
ATT_BLOCK = 256
VMEM_LIMIT = 56 * 1024 * 1024


def _rms_scale(x):
    return lax.rsqrt(jnp.mean(x * x, axis=-1, keepdims=True) + EPS)


def _sigmoid(x):
    return jax.nn.sigmoid(x)


def _softplus(x):
    return jnp.maximum(x, 0.0) + jnp.log1p(jnp.exp(-jnp.abs(x)))


def _a_layer_kernel(x_ref, st0_ref, gn_ref, win_ref, cw_ref, cb_ref, wri_ref, bri_ref, lam_ref,
                    wout_ref, out_ref, st_out_ref, ug_s, ubuf, hcar, *, chunk):
    c = pl.program_id(1)
    w = LRU_WIDTH

    @pl.when(c == 0)
    def _():
        hcar[...] = st0_ref[0:8, :]
        ubuf[...] = st0_ref[8:16, :]

    x = x_ref[0]
    xn = (x * _rms_scale(x) * gn_ref[...]).astype(BF16)
    ug_s[...] = jnp.dot(xn, win_ref[0].astype(BF16), preferred_element_type=F32)

    u = ug_s[:, 0:w]
    tail = ubuf[...]
    row8 = lax.broadcasted_iota(jnp.int32, (V7X_SUBLANES, w), 0)

    def delayed(k):
        r = pltpu.roll(u, k, 0)
        top = jnp.where(row8 >= k, r[0:V7X_SUBLANES, :], pltpu.roll(tail, k, 0))
        return jnp.concatenate([top, r[V7X_SUBLANES:, :]], axis=0)

    uc = (cb_ref[...] + cw_ref[0:1, :] * delayed(3) + cw_ref[1:2, :] * delayed(2)
          + cw_ref[2:3, :] * delayed(1) + cw_ref[3:4, :] * u)
    ubuf[...] = u[chunk - V7X_SUBLANES:chunk, :]

    ucb = uc.astype(BF16)
    neg_c_sp = (-LRU_C * LOG2E) * _softplus(-lam_ref[...])
    tiles = chunk // V7X_SUBLANES
    row = lax.broadcasted_iota(jnp.int32, (1, V7X_SUBLANES, LRU_BLOCK), 1)
    ys = []
    for n in range(LRU_BLOCKS):
        lo, hi = n * LRU_BLOCK, (n + 1) * LRU_BLOCK
        ri = jnp.dot(ucb[:, lo:hi], wri_ref[n].astype(BF16), preferred_element_type=F32) + bri_ref[n]
        r = _sigmoid(ri[:, 0:LRU_BLOCK])
        i = _sigmoid(ri[:, LRU_BLOCK:2 * LRU_BLOCK])
        a = jnp.exp2(r * neg_c_sp[:, lo:hi])
        om = 1.0 - a * a
        b = om * lax.rsqrt(jnp.maximum(om, SQRT_FLOOR)) * i * uc[:, lo:hi]
        a = a.reshape(tiles, V7X_SUBLANES, LRU_BLOCK)
        b = b.reshape(tiles, V7X_SUBLANES, LRU_BLOCK)
        a_in = jnp.where(row == 0, a, 0.0)
        a1 = jnp.where(row == 0, 0.0, a)
        a2 = a1 * pltpu.roll(a1, 1, 1)
        a4 = a2 * pltpu.roll(a2, 2, 1)
        carry = hcar[0:1, lo:hi]
        hs = []
        for j in range(tiles):
            h = b[j] + a_in[j] * carry
            h = a1[j] * pltpu.roll(h, 1, 0) + h
            h = a2[j] * pltpu.roll(h, 2, 0) + h
            h = a4[j] * pltpu.roll(h, 4, 0) + h
            hs.append(h)
            carry = h[V7X_SUBLANES - 1:V7X_SUBLANES, :]
        hcar[:, lo:hi] = jnp.broadcast_to(carry, (V7X_SUBLANES, LRU_BLOCK))
        gate = ug_s[:, w + lo:w + hi]
        ys.append((jnp.concatenate(hs, axis=0) * (gate * _sigmoid(gate))).astype(BF16))
    y = jnp.concatenate(ys, axis=1)
    out_ref[0] = x + jnp.dot(y, wout_ref[0].astype(BF16), preferred_element_type=F32)

    @pl.when(c == pl.num_programs(1) - 1)
    def _():
        st_out_ref[0, 0:8, :] = hcar[...]
        st_out_ref[0, 8:16, :] = ubuf[...]


def _a_layer_perm_kernel(x_ref, st0_ref, gn_ref, win_ref, cw_ref, cb_ref, wri_ref, bri_ref, lam_ref,
                         wout_ref, perm_ref, permt_ref, out_ref, win_bf, wri_bf, wout_bf,
                         xs, ug_s, ubuf, hcar, *, chunk, chunks_per_row):
    i_step = pl.program_id(0)

    @pl.when(i_step == 0)
    def _():
        win_bf[...] = win_ref[0].astype(BF16)
        wri_bf[...] = wri_ref[...].astype(BF16)
        wout_bf[...] = wout_ref[0].astype(BF16)
        xs[1] = jnp.zeros((chunk, D_MODEL), F32)
        ug_s[1] = jnp.zeros((chunk, 2 * LRU_WIDTH), F32)
        hcar[...] = jnp.zeros_like(hcar)
        ubuf[...] = jnp.zeros_like(ubuf)

    refs = (x_ref, st0_ref, gn_ref, win_bf, cw_ref, cb_ref, wri_bf, bri_ref, lam_ref, wout_bf,
            perm_ref, permt_ref, out_ref, xs, ug_s, ubuf, hcar)
    for parity in (0, 1):
        pl.when(lax.rem(i_step, 2) == parity)(
            functools.partial(_a_layer_perm_step, parity, *refs, chunk=chunk,
                              chunks_per_row=chunks_per_row))


def _a_layer_perm_step(slot_new, x_ref, st0_ref, gn_ref, win_ref, cw_ref, cb_ref, wri_ref, bri_ref,
                       lam_ref, wout_ref, perm_ref, permt_ref, out_ref, xs, ug_s, ubuf, hcar,
                       *, chunk, chunks_per_row):
    i_step = pl.program_id(0)
    w = LRU_WIDTH
    groups = chunk // A_GROUP
    tiles = A_GROUP // V7X_SUBLANES
    slot_old = 1 - slot_new

    row_start = lax.rem(jnp.maximum(i_step - 1, 0), chunks_per_row) == 0
    tail = jnp.where(row_start, st0_ref[8:16, :], ubuf[...])
    h_prev = jnp.where(row_start, st0_ref[0:1, :], hcar[0:1, :])

    x_new = x_ref[0]
    xs[slot_new] = x_new
    xn = (x_new * _rms_scale(x_new) * gn_ref[...]).astype(BF16)
    xp = jnp.concatenate(
        [jnp.dot(perm_ref[...], xn[g * A_GROUP:(g + 1) * A_GROUP, :],
                 preferred_element_type=F32).astype(BF16) for g in range(groups)], axis=0)

    neg_c_sp = (-LRU_C * LOG2E) * _softplus(-lam_ref[...])
    row = lax.broadcasted_iota(jnp.int32, (V7X_SUBLANES, LRU_BLOCK), 0)


    def project(col):
        ug_s[slot_new, :, col:col + LRU_BLOCK] = jnp.dot(xp, win_ref[:, col:col + LRU_BLOCK],
                                                         preferred_element_type=F32)

    def conv_and_gates(n):
        lo, hi = n * LRU_BLOCK, (n + 1) * LRU_BLOCK
        prev = [tail[5 + j:6 + j, lo:hi] for j in range(3)]
        ucs = []
        for g in range(groups):
            ug = ug_s[slot_old, g * A_GROUP:(g + 1) * A_GROUP, lo:hi]
            last = [ug[(tiles - 3 + j) * V7X_SUBLANES:(tiles - 2 + j) * V7X_SUBLANES, :]
                    for j in range(3)]
            hist = [jnp.where(row == 0, prev[j], pltpu.roll(last[j], 1, 0)) for j in range(3)]
            u_ext = jnp.concatenate(hist + [ug], axis=0)
            ucs.append(cb_ref[:, lo:hi] + cw_ref[0:1, lo:hi] * u_ext[0:A_GROUP, :]
                       + cw_ref[1:2, lo:hi] * u_ext[8:8 + A_GROUP, :]
                       + cw_ref[2:3, lo:hi] * u_ext[16:16 + A_GROUP, :] + cw_ref[3:4, lo:hi] * ug)
            prev = [t[V7X_SUBLANES - 1:V7X_SUBLANES, :] for t in last]
        ubuf[:, lo:hi] = jnp.where(row == 5, prev[0], jnp.where(row == 6, prev[1], prev[2]))
        uc = jnp.concatenate(ucs, axis=0)
        ri = jnp.dot(uc.astype(BF16), wri_ref[n], preferred_element_type=F32) + bri_ref[n]
        return uc, ri

    def recur(n, uc, ri):
        lo, hi = n * LRU_BLOCK, (n + 1) * LRU_BLOCK
        r = _sigmoid(ri[:, 0:LRU_BLOCK])
        i = _sigmoid(ri[:, LRU_BLOCK:2 * LRU_BLOCK])
        a = jnp.exp2(r * neg_c_sp[:, lo:hi])
        om = 1.0 - a * a
        b = om * lax.rsqrt(jnp.maximum(om, SQRT_FLOOR)) * i * uc
        a = a.reshape(groups, tiles, V7X_SUBLANES, LRU_BLOCK)
        b = b.reshape(groups, tiles, V7X_SUBLANES, LRU_BLOCK)
        carry = h_prev[:, lo:hi]
        hs = []
        for g in range(groups):
            h0, dec = [b[g, 0]], [a[g, 0]]
            for t in range(1, tiles):
                h0.append(a[g, t] * h0[-1] + b[g, t])
                dec.append(a[g, t] * dec[-1])
            a_end, h_end = dec[-1], h0[-1]
            a1 = jnp.where(row == 0, 0.0, a_end)
            a2 = a1 * pltpu.roll(a1, 1, 0)
            a4 = a2 * pltpu.roll(a2, 2, 0)
            h_end = h_end + jnp.where(row == 0, a_end, 0.0) * carry
            h_end = a1 * pltpu.roll(h_end, 1, 0) + h_end
            h_end = a2 * pltpu.roll(h_end, 2, 0) + h_end
            h_end = a4 * pltpu.roll(h_end, 4, 0) + h_end
            h_in = jnp.where(row == 0, carry, pltpu.roll(h_end, 1, 0))
            hs.extend(h0[t] + dec[t] * h_in for t in range(tiles))
            carry = h_end[V7X_SUBLANES - 1:V7X_SUBLANES, :]
        hcar[:, lo:hi] = jnp.broadcast_to(carry, (V7X_SUBLANES, LRU_BLOCK))
        gate = ug_s[slot_old, :, w + lo:w + hi]
        return (jnp.concatenate(hs, axis=0) * (gate * _sigmoid(gate))).astype(BF16)

    def out_project(n, yp):
        lo, hi = n * LRU_BLOCK, (n + 1) * LRU_BLOCK
        y = jnp.concatenate(
            [jnp.dot(permt_ref[...], yp[g * A_GROUP:(g + 1) * A_GROUP, :],
                     preferred_element_type=F32).astype(BF16) for g in range(groups)], axis=0)
        return jnp.dot(y, wout_ref[lo:hi, :], preferred_element_type=F32)

    out = xs[slot_old]
    mid = conv_and_gates(0)
    for n in range(LRU_BLOCKS):
        project(n * LRU_BLOCK)
        mid_next = conv_and_gates(n + 1) if n + 1 < LRU_BLOCKS else None
        project(w + n * LRU_BLOCK)
        out = out + out_project(n, recur(n, *mid))
        mid = mid_next
    out_ref[0] = out


def _a_layer_perm(x, st0, p, *, chunk):
    bsz, t, d = x.shape
    assert t % chunk == 0 and chunk % A_GROUP == 0
    w = LRU_WIDTH
    nc = t // chunk
    steps = bsz * nc
    const2 = lambda i: (0, 0)
    const3 = lambda i: (0, 0, 0)
    layer3 = lambda i: (p["layer"], 0, 0)
    resident = dict(pipeline_mode=pl.Buffered(1))

    def chunk_in(i):
        j = jnp.minimum(i, steps - 1)
        return (j // nc, j % nc, 0)

    def chunk_out(i):
        j = jnp.maximum(i - 1, 0)
        return (j // nc, j % nc, 0)

    r = jnp.arange(A_GROUP)
    src = A_SEG * (r % V7X_SUBLANES) + r // V7X_SUBLANES
    perm = (src[:, None] == jnp.arange(A_GROUP)[None, :]).astype(BF16)
    return pl.pallas_call(
        functools.partial(_a_layer_perm_kernel, chunk=chunk, chunks_per_row=nc),
        grid=(steps + 1,),
        in_specs=[
            pl.BlockSpec((1, chunk, d), chunk_in),
            pl.BlockSpec((16, w), const2),
            pl.BlockSpec((1, d), const2),
            pl.BlockSpec((1, d, 2 * w), layer3, **resident),
            pl.BlockSpec((CONV_WIDTH, w), const2),
            pl.BlockSpec((1, w), const2),
            pl.BlockSpec((LRU_BLOCKS, LRU_BLOCK, 2 * LRU_BLOCK), const3, **resident),
            pl.BlockSpec((LRU_BLOCKS, 1, 2 * LRU_BLOCK), const3),
            pl.BlockSpec((1, w), const2),
            pl.BlockSpec((1, w, d), layer3, **resident),
            pl.BlockSpec((A_GROUP, A_GROUP), const2),
            pl.BlockSpec((A_GROUP, A_GROUP), const2),
        ],
        out_specs=pl.BlockSpec((1, chunk, d), chunk_out),
        out_shape=jax.ShapeDtypeStruct((bsz, t, d), F32),
        scratch_shapes=[
            pltpu.VMEM((d, 2 * w), BF16),
            pltpu.VMEM((LRU_BLOCKS, LRU_BLOCK, 2 * LRU_BLOCK), BF16),
            pltpu.VMEM((w, d), BF16),
            pltpu.VMEM((2, chunk, d), F32),
            pltpu.VMEM((2, chunk, 2 * w), F32),
            pltpu.VMEM((V7X_SUBLANES, w), F32),
            pltpu.VMEM((V7X_SUBLANES, w), F32),
        ],
        compiler_params=pltpu.CompilerParams(
            dimension_semantics=("arbitrary",), vmem_limit_bytes=VMEM_LIMIT),
        name="rglru_layer",
    )(x, st0, p["gn"], p["win"], p["cw"], p["cb"], p["wri"], p["bri"], p["lam"], p["wout"],
      perm, perm.T)


def _a_layer(x, st0, p, *, chunk):
    bsz, t, d = x.shape
    assert t % chunk == 0 and chunk % V7X_SUBLANES == 0 and chunk >= 8
    w = LRU_WIDTH
    const2 = lambda b, c: (0, 0)
    const3 = lambda b, c: (0, 0, 0)
    layer3 = lambda b, c: (p["layer"], 0, 0)
    in_specs = [
        pl.BlockSpec((1, chunk, d), lambda b, c: (b, c, 0)),
        pl.BlockSpec((16, w), const2),
        pl.BlockSpec((1, d), const2),
        pl.BlockSpec((1, d, 2 * w), layer3),
        pl.BlockSpec((CONV_WIDTH, w), const2),
        pl.BlockSpec((1, w), const2),
        pl.BlockSpec((LRU_BLOCKS, LRU_BLOCK, 2 * LRU_BLOCK), const3),
        pl.BlockSpec((LRU_BLOCKS, 1, 2 * LRU_BLOCK), const3),
        pl.BlockSpec((1, w), const2),
        pl.BlockSpec((1, w, d), layer3),
    ]
    out_shape = [jax.ShapeDtypeStruct((bsz, t, d), F32), jax.ShapeDtypeStruct((bsz, 16, w), F32)]
    out_specs = [pl.BlockSpec((1, chunk, d), lambda b, c: (b, c, 0)),
                 pl.BlockSpec((1, 16, w), lambda b, c: (b, 0, 0))]
    scratch = [
        pltpu.VMEM((chunk, 2 * w), F32),
        pltpu.VMEM((V7X_SUBLANES, w), F32),
        pltpu.VMEM((V7X_SUBLANES, w), F32),
    ]
    return pl.pallas_call(
        functools.partial(_a_layer_kernel, chunk=chunk),
        grid=(bsz, t // chunk),
        in_specs=in_specs,
        out_specs=out_specs,
        out_shape=out_shape,
        scratch_shapes=scratch,
        compiler_params=pltpu.CompilerParams(
            dimension_semantics=("arbitrary", "arbitrary"), vmem_limit_bytes=VMEM_LIMIT),
        name="rglru_layer_state",
    )(x, st0, p["gn"], p["win"], p["cw"], p["cb"], p["wri"], p["bri"], p["lam"], p["wout"])


def _kv_kernel(h_ref, c0_ref, gn_ref, wkv_ref, wf_ref, bf_ref, gk_ref, *rest,
               chunk, kblock, n_valid, round_extra):
    if round_extra:
        (bwin_ref, bwout_ref, k_ref, ka_ref, vt_ref, qa_ref, cfin_ref, bwin_bf_ref, bwout_bf_ref,
         wk_ref, wv_ref, ccar) = rest
        bwin_bf_ref[...] = bwin_ref[...].astype(BF16)
        bwout_bf_ref[...] = bwout_ref[...].astype(BF16)
    else:
        k_ref, ka_ref, vt_ref, qa_ref, cfin_ref, wk_ref, wv_ref, ccar = rest
    c = pl.program_id(1)

    @pl.when((pl.program_id(0) == 0) & (c == 0))
    def _():
        wk_ref[...] = wkv_ref[:, 0:ATTN_WIDTH].astype(BF16)
        wv_ref[...] = wkv_ref[:, ATTN_WIDTH:2 * ATTN_WIDTH].astype(BF16)

    @pl.when(c == 0)
    def _():
        ccar[...] = c0_ref[...]

    x = h_ref[0]
    xn = (x * _rms_scale(x) * gn_ref[...]).astype(BF16)

    f = jnp.dot(xn, wf_ref[...], preferred_element_type=F32) + bf_ref[...]
    k = jnp.dot(xn, wk_ref[...], preferred_element_type=F32)
    v = jnp.dot(xn, wv_ref[...], preferred_element_type=F32)
    log_f = jnp.minimum(f, 0.0) - jnp.log1p(jnp.exp(-jnp.abs(f)))
    tiles = chunk // V7X_SUBLANES
    part = log_f.reshape(tiles, V7X_SUBLANES, V7X_LANES)
    row_in_tile = lax.broadcasted_iota(jnp.int32, (1, V7X_SUBLANES, V7X_LANES), 1)
    for k_step in (1, 2, 4):
        part = part + jnp.where(row_in_tile >= k_step, pltpu.roll(part, k_step, 1), 0.0)
    carry = ccar[0:1, :]
    cums = []
    for j in range(tiles):
        cums.append(part[j] + carry)
        carry = cums[-1][V7X_SUBLANES - 1:V7X_SUBLANES, :]
    cum = jnp.concatenate(cums, axis=0)
    row = lax.broadcasted_iota(jnp.int32, (chunk, V7X_LANES), 0)
    last = n_valid - 1 if n_valid < chunk else chunk - 1
    ccar[...] = jnp.broadcast_to(cum[last:last + 1, :], (V7X_SUBLANES, V7X_LANES))

    cum2 = cum * LOG2E
    hi = cum2.astype(BF16).astype(F32)
    mid = (cum2 - hi).astype(BF16).astype(F32)
    lo = (cum2 - hi - mid).astype(BF16).astype(F32)
    sub = lax.broadcasted_iota(jnp.int32, (chunk, V7X_LANES), 1) & (AUG_LANES - 1)
    one = jnp.ones_like(cum)
    zero = jnp.zeros_like(cum)
    qa = jnp.where(sub < 3, one, jnp.where(sub == 3, hi, jnp.where(sub == 4, mid,
                                                                  jnp.where(sub == 5, lo, zero))))
    ka = jnp.where(sub == 0, -hi, jnp.where(sub == 1, -mid, jnp.where(sub == 2, -lo,
                                                                      jnp.where(sub < 6, one, zero))))
    if n_valid < chunk:
        ka = jnp.where((row >= n_valid) & (sub == 0), MASKED, ka)
    qa_ref[0] = qa.astype(BF16)
    ka_ref[0] = ka.astype(BF16)

    for hd in range(HEADS):
        kh = k[:, hd * HEAD_DIM:(hd + 1) * HEAD_DIM]
        k_ref[0, hd] = (kh * _rms_scale(kh) * gk_ref[...]).astype(BF16)

    tail_row = lax.broadcasted_iota(jnp.int32, (V_ROWS - HEAD_DIM, kblock), 0)
    ones_rows = jnp.where(tail_row == 0, 1.0, 0.0).astype(BF16)
    for hd in range(HEADS):
        vt = v[:, hd * HEAD_DIM:(hd + 1) * HEAD_DIM].T
        for kb in range(chunk // kblock):
            vt_ref[0, hd, kb, 0:HEAD_DIM, :] = vt[:, kb * kblock:(kb + 1) * kblock].astype(BF16)
            vt_ref[0, hd, kb, HEAD_DIM:V_ROWS, :] = ones_rows

    @pl.when(c == pl.num_programs(1) - 1)
    def _():
        cfin_ref[0] = ccar[...]


def _kv_proj(h, c0, p, *, chunk, kblock, n_valid, extra=None):
    bsz, t, d = h.shape
    assert t % chunk == 0 and chunk % kblock == 0
    assert n_valid == t or t == chunk
    nkb = t // kblock
    nc = t // chunk
    const2 = lambda b, c: (0, 0)
    cpk = chunk // kblock
    in_specs = [
        pl.BlockSpec((1, chunk, d), lambda b, c: (b, c, 0)),
        pl.BlockSpec((V7X_SUBLANES, V7X_LANES), const2),
        pl.BlockSpec((1, d), const2),
        pl.BlockSpec((d, 2 * ATTN_WIDTH), const2, pipeline_mode=pl.Buffered(1)),
        pl.BlockSpec((d, V7X_LANES), const2),
        pl.BlockSpec((1, V7X_LANES), const2),
        pl.BlockSpec((1, HEAD_DIM), const2),
    ]
    operands = [h, c0, p["gn"], p["wkv"], p["wf"], p["bf"], p["gk"]]
    extra_specs, extra_shapes = [], []
    if extra is not None:
        for wgt in extra:
            n_l, rows, cols = wgt.shape
            assert rows % (bsz * nc) == 0 and (rows // (bsz * nc)) % 16 == 0
            slab = pl.BlockSpec((n_l, rows // (bsz * nc), cols), lambda b, c: (0, b * nc + c, 0))
            in_specs.append(slab)
            operands.append(wgt)
            extra_specs.append(slab)
            extra_shapes.append(jax.ShapeDtypeStruct(wgt.shape, BF16))
    outs = pl.pallas_call(
        functools.partial(_kv_kernel, chunk=chunk, kblock=kblock, n_valid=n_valid,
                          round_extra=extra is not None),
        grid=(bsz, nc),
        in_specs=in_specs,
        out_specs=[
            pl.BlockSpec((1, HEADS, chunk, HEAD_DIM), lambda b, c: (b, 0, c, 0)),
            pl.BlockSpec((1, chunk, V7X_LANES), lambda b, c: (b, c, 0)),
            pl.BlockSpec((1, HEADS, cpk, V_ROWS, kblock), lambda b, c: (b, 0, c, 0, 0)),
            pl.BlockSpec((1, chunk, V7X_LANES), lambda b, c: (b, c, 0)),
            pl.BlockSpec((1, V7X_SUBLANES, V7X_LANES), lambda b, c: (b, 0, 0)),
        ] + extra_specs,
        out_shape=[
            jax.ShapeDtypeStruct((bsz, HEADS, t, HEAD_DIM), BF16),
            jax.ShapeDtypeStruct((bsz, t, V7X_LANES), BF16),
            jax.ShapeDtypeStruct((bsz, HEADS, nkb, V_ROWS, kblock), BF16),
            jax.ShapeDtypeStruct((bsz, t, V7X_LANES), BF16),
            jax.ShapeDtypeStruct((bsz, V7X_SUBLANES, V7X_LANES), F32),
        ] + extra_shapes,
        scratch_shapes=[pltpu.VMEM((d, ATTN_WIDTH), BF16), pltpu.VMEM((d, ATTN_WIDTH), BF16),
                        pltpu.VMEM((V7X_SUBLANES, V7X_LANES), F32)],
        compiler_params=pltpu.CompilerParams(
            dimension_semantics=("arbitrary", "arbitrary"), vmem_limit_bytes=VMEM_LIMIT),
        name="shared_kv" if n_valid == t else "shared_kv_meta",
    )(*operands)
    return outs


def _b_layer_kernel(x_ref, qa_ref, k_ref, ka_ref, vt_ref, km_ref, kam_ref, vtm_ref,
                    gn_ref, win_ref, gq_ref, wout_ref, out_ref,
                    qg_s, qt_s, ot_s, m_s, *, bq, bk):
    qi = pl.program_id(1)
    aw = ATTN_WIDTH
    scale = HEAD_DIM ** -0.5 * LOG2E
    sub_blocks = bq // bk

    x = x_ref[0]
    xn = (x * _rms_scale(x) * gn_ref[...]).astype(BF16)
    qg_s[...] = jnp.dot(xn, win_ref[0], preferred_element_type=F32)

    qa = qa_ref[0].astype(F32)
    lane_head = lax.shift_right_logical(
        lax.broadcasted_iota(jnp.int32, (bq, V7X_LANES), 1), AUG_LANES.bit_length() - 1)
    for hd in range(HEADS):
        qh = qg_s[:, hd * HEAD_DIM:(hd + 1) * HEAD_DIM]
        qn = qh * _rms_scale(qh) * (gq_ref[...] * scale)
        qt_s[hd, 0:HEAD_DIM, :] = qn.T.astype(BF16)
        qt_s[hd, HEAD_DIM:2 * HEAD_DIM, :] = jnp.where(lane_head == hd, qa, 0.0).T.astype(BF16)

    def scores(hd, kj, q0):
        r0 = pl.multiple_of(kj * bk, bk)
        k_aug = jnp.concatenate([k_ref[0, hd, pl.ds(r0, bk), :], ka_ref[0, pl.ds(r0, bk), :]],
                                axis=1)
        return jnp.dot(k_aug, qt_s[hd, :, q0:bq], preferred_element_type=F32)

    def online_update(hd, s, vt_blk, q0):
        m_prev = m_s[hd, :, q0:bq]
        m_next = jnp.maximum(m_prev, jnp.max(s, axis=0, keepdims=True))
        alpha = jnp.exp2(m_prev - m_next)
        p = jnp.exp2(s - m_next).astype(BF16)
        ot_s[hd, :, q0:bq] = alpha * ot_s[hd, :, q0:bq] + jnp.dot(vt_blk, p,
                                                                  preferred_element_type=F32)
        m_s[hd, :, q0:bq] = m_next


    kj = qi * sub_blocks
    r0 = pl.multiple_of(kj * bk, bk)
    n_meta = N_META
    ka_blk = jnp.concatenate([kam_ref[0, 0:n_meta, :], ka_ref[0, pl.ds(r0, bk), :]], axis=0)
    key_row = lax.broadcasted_iota(jnp.int32, (n_meta + bk, bq), 0) - n_meta
    query_col = lax.broadcasted_iota(jnp.int32, (n_meta + bk, bq), 1)
    causal = key_row <= query_col
    meta_pad = jnp.zeros((vtm_ref.shape[-1] - n_meta, bq), BF16)
    ss = []
    for hd in range(HEADS):
        k_aug = jnp.concatenate(
            [jnp.concatenate([km_ref[0, hd, 0:n_meta, :], k_ref[0, hd, pl.ds(r0, bk), :]], axis=0),
             ka_blk], axis=1)
        ss.append(jnp.dot(k_aug, qt_s[hd], preferred_element_type=F32))
    for hd in range(HEADS):
        s = jnp.where(causal, ss[hd], MASKED)
        m0 = jnp.max(s, axis=0, keepdims=True)
        m_s[hd] = m0
        p = jnp.exp2(s - m0).astype(BF16)
        ot_s[hd] = (jnp.dot(vt_ref[0, hd, kj], p[n_meta:, :], preferred_element_type=F32)
                    + jnp.dot(vtm_ref[0, hd, 0], jnp.concatenate([p[0:n_meta, :], meta_pad], axis=0),
                              preferred_element_type=F32))

    for d in range(1, sub_blocks):
        q0 = d * bk
        kj = qi * sub_blocks + d
        key_row = lax.broadcasted_iota(jnp.int32, (bk, bq - q0), 0)
        query_col = lax.broadcasted_iota(jnp.int32, (bk, bq - q0), 1)
        causal = key_row <= query_col
        ss = [scores(hd, kj, q0) for hd in range(HEADS)]
        for hd in range(HEADS):
            online_update(hd, jnp.where(causal, ss[hd], MASKED), vt_ref[0, hd, kj], q0)

    def kv_body(step, carry):
        ss = [scores(hd, step * sub_blocks, 0) for hd in range(HEADS)]
        for d in range(sub_blocks):
            kj = step * sub_blocks + d
            ss_next = []
            for hd in range(HEADS):
                if d + 1 < sub_blocks:
                    ss_next.append(scores(hd, kj + 1, 0))
                online_update(hd, ss[hd], vt_ref[0, hd, kj], 0)
            ss = ss_next
        return carry

    lax.fori_loop(0, qi, kv_body, 0)

    ys = []
    for hd in range(HEADS):
        gate = qg_s[:, aw + hd * HEAD_DIM:aw + (hd + 1) * HEAD_DIM]
        o = (ot_s[hd, 0:HEAD_DIM, :] / ot_s[hd, HEAD_DIM:HEAD_DIM + 1, :]).T
        ys.append((o * (gate * _sigmoid(gate))).astype(BF16))
    y = jnp.concatenate(ys, axis=1)
    out_ref[0] = x + jnp.dot(y, wout_ref[0], preferred_element_type=F32)


def _b_layer(x, qa, k, ka, vt, km, kam, vtm, p, *, bq, bk):
    bsz, t, d = x.shape
    assert t % bq == 0 and bq % bk == 0
    nkb = t // bk
    aw = ATTN_WIDTH
    const2 = lambda b, q: (0, 0)
    layer3 = lambda b, q: (p["layer"], 0, 0)
    resident = dict(pipeline_mode=pl.Buffered(1))
    return pl.pallas_call(
        functools.partial(_b_layer_kernel, bq=bq, bk=bk),
        grid=(bsz, t // bq),
        in_specs=[
            pl.BlockSpec((1, bq, d), lambda b, q: (b, q, 0)),
            pl.BlockSpec((1, bq, V7X_LANES), lambda b, q: (b, q, 0)),
            pl.BlockSpec((1, HEADS, t, HEAD_DIM), lambda b, q: (b, 0, 0, 0)),
            pl.BlockSpec((1, t, V7X_LANES), lambda b, q: (b, 0, 0)),
            pl.BlockSpec((1, HEADS, nkb, V_ROWS, bk), lambda b, q: (b, 0, 0, 0, 0)),
            pl.BlockSpec((1, HEADS, V7X_LANES, HEAD_DIM), lambda b, q: (0, 0, 0, 0)),
            pl.BlockSpec((1, V7X_LANES, V7X_LANES), lambda b, q: (0, 0, 0)),
            pl.BlockSpec((1, HEADS, 1, V_ROWS, V7X_LANES), lambda b, q: (0, 0, 0, 0, 0)),
            pl.BlockSpec((1, d), const2),
            pl.BlockSpec((1, d, 2 * aw), layer3, **resident),
            pl.BlockSpec((1, HEAD_DIM), const2),
            pl.BlockSpec((1, aw, d), layer3, **resident),
        ],
        out_specs=pl.BlockSpec((1, bq, d), lambda b, q: (b, q, 0)),
        out_shape=jax.ShapeDtypeStruct((bsz, t, d), F32),
        scratch_shapes=[
            pltpu.VMEM((bq, 2 * aw), F32),
            pltpu.VMEM((HEADS, 2 * HEAD_DIM, bq), BF16),
            pltpu.VMEM((HEADS, V_ROWS, bq), F32),
            pltpu.VMEM((HEADS, 1, bq), F32),
        ],
        compiler_params=pltpu.CompilerParams(
            dimension_semantics=("arbitrary", "arbitrary"), vmem_limit_bytes=VMEM_LIMIT),
        name="fox_attention_layer",
    )(x, qa, k, ka, vt, km, kam, vtm, p["gn"], p["win"], p["gq"], p["wout"])


def _a_params(l, a_norm, a_w_in, a_conv_w, a_conv_b, a_w_r, a_b_r, a_w_i, a_b_i, a_lambda, a_w_out):
    wri = jnp.concatenate([a_w_r[l], a_w_i[l]], axis=-1).astype(F32)
    bri = jnp.concatenate([a_b_r[l].reshape(LRU_BLOCKS, 1, LRU_BLOCK),
                           a_b_i[l].reshape(LRU_BLOCKS, 1, LRU_BLOCK)], axis=-1).astype(F32)
    return dict(layer=l, gn=a_norm[l].reshape(1, -1).astype(F32), win=a_w_in.astype(F32),
                cw=a_conv_w[l].astype(F32), cb=a_conv_b[l].reshape(1, -1).astype(F32),
                wri=wri, bri=bri, lam=a_lambda[l].reshape(1, -1).astype(F32),
                wout=a_w_out.astype(F32))


def _kv_params(kv_norm, w_kv, b_f, k_norm):
    aw = ATTN_WIDTH
    wf = jnp.repeat(w_kv[:, 2 * aw:2 * aw + HEADS].astype(F32), AUG_LANES, axis=1)
    bfp = jnp.repeat(b_f.astype(F32), AUG_LANES).reshape(1, V7X_LANES)
    return dict(gn=kv_norm.reshape(1, -1).astype(F32), wkv=w_kv.astype(F32), wf=wf.astype(BF16),
                bf=bfp, gk=k_norm.reshape(1, -1).astype(F32))


def kernel(x, meta_tokens, a_norm, a_w_in, a_conv_w, a_conv_b, a_w_r, a_b_r, a_w_i, a_b_i, a_lambda,
           a_w_out, kv_norm, w_kv, b_f, k_norm, b_norm, b_w_in, q_norm, b_w_out):
    n_a = a_norm.shape[0]
    n_b = b_norm.shape[0]
    h = x.astype(F32)
    hm = meta_tokens[None].astype(F32)
    zero_state = jnp.zeros((16, LRU_WIDTH), F32)

    for l in range(n_a):
        p = _a_params(l, a_norm, a_w_in, a_conv_w, a_conv_b, a_w_r, a_b_r, a_w_i, a_b_i, a_lambda,
                      a_w_out)
        hm, st = _a_layer(hm, zero_state, p, chunk=N_META)
        h = _a_layer_perm(h, st[0], p, chunk=A_CHUNK)

    pkv = _kv_params(kv_norm, w_kv, b_f, k_norm)
    hm_pad = jnp.pad(hm, ((0, 0), (0, V7X_LANES - N_META), (0, 0)))
    c_zero = jnp.zeros((V7X_SUBLANES, V7X_LANES), F32)
    km, kam, vtm, _, c_meta = _kv_proj(hm_pad, c_zero, pkv, chunk=V7X_LANES, kblock=V7X_LANES,
                                       n_valid=N_META)
    k, ka, vt, qa, _, b_win, b_wout = _kv_proj(h, c_meta[0], pkv, chunk=KV_CHUNK, kblock=ATT_BLOCK,
                                               n_valid=h.shape[1],
                                               extra=(b_w_in.astype(F32), b_w_out.astype(F32)))

    for j in range(n_b):
        p = dict(layer=j, gn=b_norm[j].reshape(1, -1).astype(F32), win=b_win,
                 gq=q_norm[j].reshape(1, -1).astype(F32), wout=b_wout)
        h = _b_layer(h, qa, k, ka, vt, km, kam, vtm, p, bq=ATT_QUERIES, bk=ATT_BLOCK)
    return h
```

```python
import functools

import jax
import jax.numpy as jnp
from jax import lax
from jax.experimental import pallas as pl
from jax.experimental.pallas import tpu as pltpu

F32 = jnp.float32
BF16 = jnp.bfloat16

D_MODEL = 1024
LRU_WIDTH = 1024
LRU_BLOCKS = 4
LRU_BLOCK = LRU_WIDTH // LRU_BLOCKS
CONV_WIDTH = 4
LRU_C = 8.0
HEADS = 8
HEAD_DIM = 128
ATTN_WIDTH = HEADS * HEAD_DIM
N_META = 16
EPS = 1e-6

V7X_SUBLANES = 8
V7X_LANES = 128
AUG_LANES = V7X_LANES // HEADS
MASKED = -1e30
SQRT_FLOOR = 1e-37
LOG2E = 1.4426950408889634
V_ROWS = HEAD_DIM + 16

A_CHUNK = 512
A_GROUP = 256
A_SEG = A_GROUP // V7X_SUBLANES
KV_CHUNK = 512
ATT_QUERIES = 512
ATT_BLOCK = 256
VMEM_LIMIT = 56 * 1024 * 1024


def _rms_scale(x):
    return lax.rsqrt(jnp.mean(x * x, axis=-1, keepdims=True) + EPS)


def _sigmoid(x):
    return jax.nn.sigmoid(x)


def _softplus(x):
    return jnp.maximum(x, 0.0) + jnp.log1p(jnp.exp(-jnp.abs(x)))


def _a_layer_kernel(x_ref, st0_ref, gn_ref, win_ref, cw_ref, cb_ref, wri_ref, bri_ref, lam_ref,
                    wout_ref, out_ref, st_out_ref, ug_s, ubuf, hcar, *, chunk):
    c = pl.program_id(1)
    w = LRU_WIDTH

    @pl.when(c == 0)
    def _():
        hcar[...] = st0_ref[0:8, :]
        ubuf[...] = st0_ref[8:16, :]

    x = x_ref[0]
    xn = (x * _rms_scale(x) * gn_ref[...]).astype(BF16)
    ug_s[...] = jnp.dot(xn, win_ref[0].astype(BF16), preferred_element_type=F32)

    u = ug_s[:, 0:w]
    tail = ubuf[...]
    row8 = lax.broadcasted_iota(jnp.int32, (V7X_SUBLANES, w), 0)

    def delayed(k):
        r = pltpu.roll(u, k, 0)
        top = jnp.where(row8 >= k, r[0:V7X_SUBLANES, :], pltpu.roll(tail, k, 0))
        return jnp.concatenate([top, r[V7X_SUBLANES:, :]], axis=0)

    uc = (cb_ref[...] + cw_ref[0:1, :] * delayed(3) + cw_ref[1:2, :] * delayed(2)
          + cw_ref[2:3, :] * delayed(1) + cw_ref[3:4, :] * u)
    ubuf[...] = u[chunk - V7X_SUBLANES:chunk, :]

    ucb = uc.astype(BF16)
    neg_c_sp = (-LRU_C * LOG2E) * _softplus(-lam_ref[...])
    tiles = chunk // V7X_SUBLANES
    row = lax.broadcasted_iota(jnp.int32, (1, V7X_SUBLANES, LRU_BLOCK), 1)
    ys = []
    for n in range(LRU_BLOCKS):
        lo, hi = n * LRU_BLOCK, (n + 1) * LRU_BLOCK
        ri = jnp.dot(ucb[:, lo:hi], wri_ref[n].astype(BF16), preferred_element_type=F32) + bri_ref[n]
        r = _sigmoid(ri[:, 0:LRU_BLOCK])
        i = _sigmoid(ri[:, LRU_BLOCK:2 * LRU_BLOCK])
        a = jnp.exp2(r * neg_c_sp[:, lo:hi])
        om = 1.0 - a * a
        b = om * lax.rsqrt(jnp.maximum(om, SQRT_FLOOR)) * i * uc[:, lo:hi]
        a = a.reshape(tiles, V7X_SUBLANES, LRU_BLOCK)
        b = b.reshape(tiles, V7X_SUBLANES, LRU_BLOCK)
        a_in = jnp.where(row == 0, a, 0.0)
        a1 = jnp.where(row == 0, 0.0, a)
        a2 = a1 * pltpu.roll(a1, 1, 1)
        a4 = a2 * pltpu.roll(a2, 2, 1)
        carry = hcar[0:1, lo:hi]
        hs = []
        for j in range(tiles):
            h = b[j] + a_in[j] * carry
            h = a1[j] * pltpu.roll(h, 1, 0) + h
            h = a2[j] * pltpu.roll(h, 2, 0) + h
            h = a4[j] * pltpu.roll(h, 4, 0) + h
            hs.append(h)
            carry = h[V7X_SUBLANES - 1:V7X_SUBLANES, :]
        hcar[:, lo:hi] = jnp.broadcast_to(carry, (V7X_SUBLANES, LRU_BLOCK))
        gate = ug_s[:, w + lo:w + hi]
        ys.append((jnp.concatenate(hs, axis=0) * (gate * _sigmoid(gate))).astype(BF16))
    y = jnp.concatenate(ys, axis=1)
    out_ref[0] = x + jnp.dot(y, wout_ref[0].astype(BF16), preferred_element_type=F32)

    @pl.when(c == pl.num_programs(1) - 1)
    def _():
        st_out_ref[0, 0:8, :] = hcar[...]
        st_out_ref[0, 8:16, :] = ubuf[...]


def _a_layer_perm_kernel(x_ref, st0_ref, gn_ref, win_ref, cw_ref, cb_ref, wri_ref, bri_ref, lam_ref,
                         wout_ref, perm_ref, permt_ref, out_ref, win_bf, wri_bf, wout_bf,
                         xs, ug_s, ubuf, hcar, *, chunk, chunks_per_row):
    i_step = pl.program_id(0)

    @pl.when(i_step == 0)
    def _():
        win_bf[...] = win_ref[0].astype(BF16)
        wri_bf[...] = wri_ref[...].astype(BF16)
        wout_bf[...] = wout_ref[0].astype(BF16)
        xs[1] = jnp.zeros((chunk, D_MODEL), F32)
        ug_s[1] = jnp.zeros((chunk, 2 * LRU_WIDTH), F32)
        hcar[...] = jnp.zeros_like(hcar)
        ubuf[...] = jnp.zeros_like(ubuf)

    refs = (x_ref, st0_ref, gn_ref, win_bf, cw_ref, cb_ref, wri_bf, bri_ref, lam_ref, wout_bf,
            perm_ref, permt_ref, out_ref, xs, ug_s, ubuf, hcar)
    for parity in (0, 1):
        pl.when(lax.rem(i_step, 2) == parity)(
            functools.partial(_a_layer_perm_step, parity, *refs, chunk=chunk,
                              chunks_per_row=chunks_per_row))


def _a_layer_perm_step(slot_new, x_ref, st0_ref, gn_ref, win_ref, cw_ref, cb_ref, wri_ref, bri_ref,
                       lam_ref, wout_ref, perm_ref, permt_ref, out_ref, xs, ug_s, ubuf, hcar,
                       *, chunk, chunks_per_row):
    i_step = pl.program_id(0)
    w = LRU_WIDTH
    groups = chunk // A_GROUP
    tiles = A_GROUP // V7X_SUBLANES
    slot_old = 1 - slot_new

    row_start = lax.rem(jnp.maximum(i_step - 1, 0), chunks_per_row) == 0
    tail = jnp.where(row_start, st0_ref[8:16, :], ubuf[...])
    h_prev = jnp.where(row_start, st0_ref[0:1, :], hcar[0:1, :])

    x_new = x_ref[0]
    xs[slot_new] = x_new
    xn = (x_new * _rms_scale(x_new) * gn_ref[...]).astype(BF16)
    xp = jnp.concatenate(
        [jnp.dot(perm_ref[...], xn[g * A_GROUP:(g + 1) * A_GROUP, :],
                 preferred_element_type=F32).astype(BF16) for g in range(groups)], axis=0)

    neg_c_sp = (-LRU_C * LOG2E) * _softplus(-lam_ref[...])
    row = lax.broadcasted_iota(jnp.int32, (V7X_SUBLANES, LRU_BLOCK), 0)


    def project(col):
        ug_s[slot_new, :, col:col + LRU_BLOCK] = jnp.dot(xp, win_ref[:, col:col + LRU_BLOCK],
                                                         preferred_element_type=F32)

    def conv_and_gates(n):
        lo, hi = n * LRU_BLOCK, (n + 1) * LRU_BLOCK
        prev = [tail[5 + j:6 + j, lo:hi] for j in range(3)]
        ucs = []
        for g in range(groups):
            ug = ug_s[slot_old, g * A_GROUP:(g + 1) * A_GROUP, lo:hi]
            last = [ug[(tiles - 3 + j) * V7X_SUBLANES:(tiles - 2 + j) * V7X_SUBLANES, :]
                    for j in range(3)]
            hist = [jnp.where(row == 0, prev[j], pltpu.roll(last[j], 1, 0)) for j in range(3)]
            u_ext = jnp.concatenate(hist + [ug], axis=0)
            ucs.append(cb_ref[:, lo:hi] + cw_ref[0:1, lo:hi] * u_ext[0:A_GROUP, :]
                       + cw_ref[1:2, lo:hi] * u_ext[8:8 + A_GROUP, :]
                       + cw_ref[2:3, lo:hi] * u_ext[16:16 + A_GROUP, :] + cw_ref[3:4, lo:hi] * ug)
            prev = [t[V7X_SUBLANES - 1:V7X_SUBLANES, :] for t in last]
        ubuf[:, lo:hi] = jnp.where(row == 5, prev[0], jnp.where(row == 6, prev[1], prev[2]))
        uc = jnp.concatenate(ucs, axis=0)
        ri = jnp.dot(uc.astype(BF16), wri_ref[n], preferred_element_type=F32) + bri_ref[n]
        return uc, ri

    def recur(n, uc, ri):
        lo, hi = n * LRU_BLOCK, (n + 1) * LRU_BLOCK
        r = _sigmoid(ri[:, 0:LRU_BLOCK])
        i = _sigmoid(ri[:, LRU_BLOCK:2 * LRU_BLOCK])
        a = jnp.exp2(r * neg_c_sp[:, lo:hi])
        om = 1.0 - a * a
        b = om * lax.rsqrt(jnp.maximum(om, SQRT_FLOOR)) * i * uc
        a = a.reshape(groups, tiles, V7X_SUBLANES, LRU_BLOCK)
        b = b.reshape(groups, tiles, V7X_SUBLANES, LRU_BLOCK)
        carry = h_prev[:, lo:hi]
        hs = []
        for g in range(groups):
            h0, dec = [b[g, 0]], [a[g, 0]]
            for t in range(1, tiles):
                h0.append(a[g, t] * h0[-1] + b[g, t])
                dec.append(a[g, t] * dec[-1])
            a_end, h_end = dec[-1], h0[-1]
            a1 = jnp.where(row == 0, 0.0, a_end)
            a2 = a1 * pltpu.roll(a1, 1, 0)
            a4 = a2 * pltpu.roll(a2, 2, 0)
            h_end = h_end + jnp.where(row == 0, a_end, 0.0) * carry
            h_end = a1 * pltpu.roll(h_end, 1, 0) + h_end
            h_end = a2 * pltpu.roll(h_end, 2, 0) + h_end
            h_end = a4 * pltpu.roll(h_end, 4, 0) + h_end
            h_in = jnp.where(row == 0, carry, pltpu.roll(h_end, 1, 0))
            hs.extend(h0[t] + dec[t] * h_in for t in range(tiles))
            carry = h_end[V7X_SUBLANES - 1:V7X_SUBLANES, :]
        hcar[:, lo:hi] = jnp.broadcast_to(carry, (V7X_SUBLANES, LRU_BLOCK))
        gate = ug_s[slot_old, :, w + lo:w + hi]
        return (jnp.concatenate(hs, axis=0) * (gate * _sigmoid(gate))).astype(BF16)

    def out_project(n, yp):
        lo, hi = n * LRU_BLOCK, (n + 1) * LRU_BLOCK
        y = jnp.concatenate(
            [jnp.dot(permt_ref[...], yp[g * A_GROUP:(g + 1) * A_GROUP, :],
                     preferred_element_type=F32).astype(BF16) for g in range(groups)], axis=0)
        return jnp.dot(y, wout_ref[lo:hi, :], preferred_element_type=F32)

    out = xs[slot_old]
    mid = conv_and_gates(0)
    for n in range(LRU_BLOCKS):
        project(n * LRU_BLOCK)
        mid_next = conv_and_gates(n + 1) if n + 1 < LRU_BLOCKS else None
        project(w + n * LRU_BLOCK)
        out = out + out_project(n, recur(n, *mid))
        mid = mid_next
    out_ref[0] = out


def _a_layer_perm(x, st0, p, *, chunk):
    bsz, t, d = x.shape
    assert t % chunk == 0 and chunk % A_GROUP == 0
    w = LRU_WIDTH
    nc = t // chunk
    steps = bsz * nc
    const2 = lambda i: (0, 0)
    const3 = lambda i: (0, 0, 0)
    layer3 = lambda i: (p["layer"], 0, 0)
    resident = dict(pipeline_mode=pl.Buffered(1))

    def chunk_in(i):
        j = jnp.minimum(i, steps - 1)
        return (j // nc, j % nc, 0)

    def chunk_out(i):
        j = jnp.maximum(i - 1, 0)
        return (j // nc, j % nc, 0)

    r = jnp.arange(A_GROUP)
    src = A_SEG * (r % V7X_SUBLANES) + r // V7X_SUBLANES
    perm = (src[:, None] == jnp.arange(A_GROUP)[None, :]).astype(BF16)
    return pl.pallas_call(
        functools.partial(_a_layer_perm_kernel, chunk=chunk, chunks_per_row=nc),
        grid=(steps + 1,),
        in_specs=[
            pl.BlockSpec((1, chunk, d), chunk_in),
            pl.BlockSpec((16, w), const2),
            pl.BlockSpec((1, d), const2),
            pl.BlockSpec((1, d, 2 * w), layer3, **resident),
            pl.BlockSpec((CONV_WIDTH, w), const2),
            pl.BlockSpec((1, w), const2),
            pl.BlockSpec((LRU_BLOCKS, LRU_BLOCK, 2 * LRU_BLOCK), const3, **resident),
            pl.BlockSpec((LRU_BLOCKS, 1, 2 * LRU_BLOCK), const3),
            pl.BlockSpec((1, w), const2),
            pl.BlockSpec((1, w, d), layer3, **resident),
            pl.BlockSpec((A_GROUP, A_GROUP), const2),
            pl.BlockSpec((A_GROUP, A_GROUP), const2),
        ],
        out_specs=pl.BlockSpec((1, chunk, d), chunk_out),
        out_shape=jax.ShapeDtypeStruct((bsz, t, d), F32),
        scratch_shapes=[
            pltpu.VMEM((d, 2 * w), BF16),
            pltpu.VMEM((LRU_BLOCKS, LRU_BLOCK, 2 * LRU_BLOCK), BF16),
            pltpu.VMEM((w, d), BF16),
            pltpu.VMEM((2, chunk, d), F32),
            pltpu.VMEM((2, chunk, 2 * w), F32),
            pltpu.VMEM((V7X_SUBLANES, w), F32),
            pltpu.VMEM((V7X_SUBLANES, w), F32),
        ],
        compiler_params=pltpu.CompilerParams(
            dimension_semantics=("arbitrary",), vmem_limit_bytes=VMEM_LIMIT),
        name="rglru_layer",
    )(x, st0, p["gn"], p["win"], p["cw"], p["cb"], p["wri"], p["bri"], p["lam"], p["wout"],
      perm, perm.T)


def _a_layer(x, st0, p, *, chunk):
    bsz, t, d = x.shape
    assert t % chunk == 0 and chunk % V7X_SUBLANES == 0 and chunk >= 8
    w = LRU_WIDTH
    const2 = lambda b, c: (0, 0)
    const3 = lambda b, c: (0, 0, 0)
    layer3 = lambda b, c: (p["layer"], 0, 0)
    in_specs = [
        pl.BlockSpec((1, chunk, d), lambda b, c: (b, c, 0)),
        pl.BlockSpec((16, w), const2),
        pl.BlockSpec((1, d), const2),
        pl.BlockSpec((1, d, 2 * w), layer3),
        pl.BlockSpec((CONV_WIDTH, w), const2),
        pl.BlockSpec((1, w), const2),
        pl.BlockSpec((LRU_BLOCKS, LRU_BLOCK, 2 * LRU_BLOCK), const3),
        pl.BlockSpec((LRU_BLOCKS, 1, 2 * LRU_BLOCK), const3),
        pl.BlockSpec((1, w), const2),
        pl.BlockSpec((1, w, d), layer3),
    ]
    out_shape = [jax.ShapeDtypeStruct((bsz, t, d), F32), jax.ShapeDtypeStruct((bsz, 16, w), F32)]
    out_specs = [pl.BlockSpec((1, chunk, d), lambda b, c: (b, c, 0)),
                 pl.BlockSpec((1, 16, w), lambda b, c: (b, 0, 0))]
    scratch = [
        pltpu.VMEM((chunk, 2 * w), F32),
        pltpu.VMEM((V7X_SUBLANES, w), F32),
        pltpu.VMEM((V7X_SUBLANES, w), F32),
    ]
    return pl.pallas_call(
        functools.partial(_a_layer_kernel, chunk=chunk),
        grid=(bsz, t // chunk),
        in_specs=in_specs,
        out_specs=out_specs,
        out_shape=out_shape,
        scratch_shapes=scratch,
        compiler_params=pltpu.CompilerParams(
            dimension_semantics=("arbitrary", "arbitrary"), vmem_limit_bytes=VMEM_LIMIT),
        name="rglru_layer_state",
    )(x, st0, p["gn"], p["win"], p["cw"], p["cb"], p["wri"], p["bri"], p["lam"], p["wout"])


def _kv_kernel(h_ref, c0_ref, gn_ref, wkv_ref, wf_ref, bf_ref, gk_ref, *rest,
               chunk, kblock, n_valid, round_extra):
    if round_extra:
        (bwin_ref, bwout_ref, k_ref, ka_ref, vt_ref, qa_ref, cfin_ref, bwin_bf_ref, bwout_bf_ref,
         wk_ref, wv_ref, ccar) = rest
        bwin_bf_ref[...] = bwin_ref[...].astype(BF16)
        bwout_bf_ref[...] = bwout_ref[...].astype(BF16)
    else:
        k_ref, ka_ref, vt_ref, qa_ref, cfin_ref, wk_ref, wv_ref, ccar = rest
    c = pl.program_id(1)

    @pl.when((pl.program_id(0) == 0) & (c == 0))
    def _():
        wk_ref[...] = wkv_ref[0:ATTN_WIDTH, :].T.astype(BF16)
        wv_ref[...] = wkv_ref[ATTN_WIDTH:2 * ATTN_WIDTH, :].T.astype(BF16)

    @pl.when(c == 0)
    def _():
        ccar[...] = c0_ref[...]

    x = h_ref[0]
    xn = (x * _rms_scale(x) * gn_ref[...]).astype(BF16)

    f = jnp.dot(xn, wf_ref[...], preferred_element_type=F32) + bf_ref[...]
    k = jnp.dot(xn, wk_ref[...], preferred_element_type=F32)
    v = jnp.dot(xn, wv_ref[...], preferred_element_type=F32)
    log_f = jnp.minimum(f, 0.0) - jnp.log1p(jnp.exp(-jnp.abs(f)))
    tiles = chunk // V7X_SUBLANES
    part = log_f.reshape(tiles, V7X_SUBLANES, V7X_LANES)
    row_in_tile = lax.broadcasted_iota(jnp.int32, (1, V7X_SUBLANES, V7X_LANES), 1)
    for k_step in (1, 2, 4):
        part = part + jnp.where(row_in_tile >= k_step, pltpu.roll(part, k_step, 1), 0.0)
    carry = ccar[0:1, :]
    cums = []
    for j in range(tiles):
        cums.append(part[j] + carry)
        carry = cums[-1][V7X_SUBLANES - 1:V7X_SUBLANES, :]
    cum = jnp.concatenate(cums, axis=0)
    row = lax.broadcasted_iota(jnp.int32, (chunk, V7X_LANES), 0)
    last = n_valid - 1 if n_valid < chunk else chunk - 1
    ccar[...] = jnp.broadcast_to(cum[last:last + 1, :], (V7X_SUBLANES, V7X_LANES))

    cum2 = cum * LOG2E
    hi = cum2.astype(BF16).astype(F32)
    mid = (cum2 - hi).astype(BF16).astype(F32)
    lo = (cum2 - hi - mid).astype(BF16).astype(F32)
    sub = lax.broadcasted_iota(jnp.int32, (chunk, V7X_LANES), 1) & (AUG_LANES - 1)
    one = jnp.ones_like(cum)
    zero = jnp.zeros_like(cum)
    qa = jnp.where(sub < 3, one, jnp.where(sub == 3, hi, jnp.where(sub == 4, mid,
                                                                  jnp.where(sub == 5, lo, zero))))
    ka = jnp.where(sub == 0, -hi, jnp.where(sub == 1, -mid, jnp.where(sub == 2, -lo,
                                                                      jnp.where(sub < 6, one, zero))))
    if n_valid < chunk:
        ka = jnp.where((row >= n_valid) & (sub == 0), MASKED, ka)
    qa_ref[0] = qa.astype(BF16)
    ka_ref[0] = ka.astype(BF16)

    for hd in range(HEADS):
        kh = k[:, hd * HEAD_DIM:(hd + 1) * HEAD_DIM]
        k_ref[0, hd] = (kh * _rms_scale(kh) * gk_ref[...]).astype(BF16)

    tail_row = lax.broadcasted_iota(jnp.int32, (V_ROWS - HEAD_DIM, kblock), 0)
    ones_rows = jnp.where(tail_row == 0, 1.0, 0.0).astype(BF16)
    for hd in range(HEADS):
        vt = v[:, hd * HEAD_DIM:(hd + 1) * HEAD_DIM].T
        for kb in range(chunk // kblock):
            vt_ref[0, hd, kb, 0:HEAD_DIM, :] = vt[:, kb * kblock:(kb + 1) * kblock].astype(BF16)
            vt_ref[0, hd, kb, HEAD_DIM:V_ROWS, :] = ones_rows

    @pl.when(c == pl.num_programs(1) - 1)
    def _():
        cfin_ref[0] = ccar[...]


def _kv_proj(h, c0, p, *, chunk, kblock, n_valid, extra=None):
    bsz, t, d = h.shape
    assert t % chunk == 0 and chunk % kblock == 0
    assert n_valid == t or t == chunk
    nkb = t // kblock
    nc = t // chunk
    const2 = lambda b, c: (0, 0)
    cpk = chunk // kblock
    in_specs = [
        pl.BlockSpec((1, chunk, d), lambda b, c: (b, c, 0)),
        pl.BlockSpec((V7X_SUBLANES, V7X_LANES), const2),
        pl.BlockSpec((1, d), const2),
        pl.BlockSpec((2 * ATTN_WIDTH, d), const2, pipeline_mode=pl.Buffered(1)),
        pl.BlockSpec((d, V7X_LANES), const2),
        pl.BlockSpec((1, V7X_LANES), const2),
        pl.BlockSpec((1, HEAD_DIM), const2),
    ]
    operands = [h, c0, p["gn"], p["wkv"], p["wf"], p["bf"], p["gk"]]
    extra_specs, extra_shapes = [], []
    if extra is not None:
        for wgt in extra:
            n_l, rows, cols = wgt.shape
            assert rows % (bsz * nc) == 0 and (rows // (bsz * nc)) % 16 == 0
            slab = pl.BlockSpec((n_l, rows // (bsz * nc), cols), lambda b, c: (0, b * nc + c, 0))
            in_specs.append(slab)
            operands.append(wgt)
            extra_specs.append(slab)
            extra_shapes.append(jax.ShapeDtypeStruct(wgt.shape, BF16))
    outs = pl.pallas_call(
        functools.partial(_kv_kernel, chunk=chunk, kblock=kblock, n_valid=n_valid,
                          round_extra=extra is not None),
        grid=(bsz, nc),
        in_specs=in_specs,
        out_specs=[
            pl.BlockSpec((1, HEADS, chunk, HEAD_DIM), lambda b, c: (b, 0, c, 0)),
            pl.BlockSpec((1, chunk, V7X_LANES), lambda b, c: (b, c, 0)),
            pl.BlockSpec((1, HEADS, cpk, V_ROWS, kblock), lambda b, c: (b, 0, c, 0, 0)),
            pl.BlockSpec((1, chunk, V7X_LANES), lambda b, c: (b, c, 0)),
            pl.BlockSpec((1, V7X_SUBLANES, V7X_LANES), lambda b, c: (b, 0, 0)),
        ] + extra_specs,
        out_shape=[
            jax.ShapeDtypeStruct((bsz, HEADS, t, HEAD_DIM), BF16),
            jax.ShapeDtypeStruct((bsz, t, V7X_LANES), BF16),
            jax.ShapeDtypeStruct((bsz, HEADS, nkb, V_ROWS, kblock), BF16),
            jax.ShapeDtypeStruct((bsz, t, V7X_LANES), BF16),
            jax.ShapeDtypeStruct((bsz, V7X_SUBLANES, V7X_LANES), F32),
        ] + extra_shapes,
        scratch_shapes=[pltpu.VMEM((d, ATTN_WIDTH), BF16), pltpu.VMEM((d, ATTN_WIDTH), BF16),
                        pltpu.VMEM((V7X_SUBLANES, V7X_LANES), F32)],
        compiler_params=pltpu.CompilerParams(
            dimension_semantics=("arbitrary", "arbitrary"), vmem_limit_bytes=VMEM_LIMIT),
        name="shared_kv" if n_valid == t else "shared_kv_meta",
    )(*operands)
    return outs


def _b_layer_kernel(x_ref, qa_ref, k_ref, ka_ref, vt_ref, km_ref, kam_ref, vtm_ref,
                    gn_ref, win_ref, gq_ref, wout_ref, out_ref,
                    qg_s, qt_s, ot_s, m_s, *, bq, bk):
    qi = pl.program_id(1)
    aw = ATTN_WIDTH
    scale = HEAD_DIM ** -0.5 * LOG2E
    sub_blocks = bq // bk

    x = x_ref[0]
    xn = (x * _rms_scale(x) * gn_ref[...]).astype(BF16)
    qg_s[...] = jnp.dot(xn, win_ref[0], preferred_element_type=F32)

    qa_t = qa_ref[0].astype(F32).T
    row_head = lax.shift_right_logical(
        lax.broadcasted_iota(jnp.int32, (V7X_LANES, bq), 0), AUG_LANES.bit_length() - 1)
    for hd in range(HEADS):
        qh = qg_s[:, hd * HEAD_DIM:(hd + 1) * HEAD_DIM]
        qn = qh * _rms_scale(qh) * (gq_ref[...] * scale)
        qt_s[hd, 0:HEAD_DIM, :] = qn.T.astype(BF16)
        qt_s[hd, HEAD_DIM:2 * HEAD_DIM, :] = jnp.where(row_head == hd, qa_t, 0.0).astype(BF16)

    def scores(hd, kj, q0):
        r0 = pl.multiple_of(kj * bk, bk)
        k_aug = jnp.concatenate([k_ref[0, hd, pl.ds(r0, bk), :], ka_ref[0, pl.ds(r0, bk), :]],
                                axis=1)
        return jnp.dot(k_aug, qt_s[hd, :, q0:bq], preferred_element_type=F32)

    def online_update(hd, s, vt_blk, q0):
        m_prev = m_s[hd, :, q0:bq]
        m_next = jnp.maximum(m_prev, jnp.max(s, axis=0, keepdims=True))
        alpha = jnp.exp2(m_prev - m_next)
        p = jnp.exp2(s - m_next).astype(BF16)
        ot_s[hd, :, q0:bq] = alpha * ot_s[hd, :, q0:bq] + jnp.dot(vt_blk, p,
                                                                  preferred_element_type=F32)
        m_s[hd, :, q0:bq] = m_next


    kj = qi * sub_blocks
    r0 = pl.multiple_of(kj * bk, bk)
    n_meta = N_META
    ka_blk = jnp.concatenate([kam_ref[0, 0:n_meta, :], ka_ref[0, pl.ds(r0, bk), :]], axis=0)
    key_row = lax.broadcasted_iota(jnp.int32, (n_meta + bk, bq), 0) - n_meta
    query_col = lax.broadcasted_iota(jnp.int32, (n_meta + bk, bq), 1)
    causal = key_row <= query_col
    meta_pad = jnp.zeros((vtm_ref.shape[-1] - n_meta, bq), BF16)
    ss = []
    for hd in range(HEADS):
        k_aug = jnp.concatenate(
            [jnp.concatenate([km_ref[0, hd, 0:n_meta, :], k_ref[0, hd, pl.ds(r0, bk), :]], axis=0),
             ka_blk], axis=1)
        ss.append(jnp.dot(k_aug, qt_s[hd], preferred_element_type=F32))
    for hd in range(HEADS):
        s = jnp.where(causal, ss[hd], MASKED)
        m0 = jnp.max(s, axis=0, keepdims=True)
        m_s[hd] = m0
        p = jnp.exp2(s - m0).astype(BF16)
        ot_s[hd] = (jnp.dot(vt_ref[0, hd, kj], p[n_meta:, :], preferred_element_type=F32)
                    + jnp.dot(vtm_ref[0, hd, 0], jnp.concatenate([p[0:n_meta, :], meta_pad], axis=0),
                              preferred_element_type=F32))

    for d in range(1, sub_blocks):
        q0 = d * bk
        kj = qi * sub_blocks + d
        key_row = lax.broadcasted_iota(jnp.int32, (bk, bq - q0), 0)
        query_col = lax.broadcasted_iota(jnp.int32, (bk, bq - q0), 1)
        causal = key_row <= query_col
        ss = [scores(hd, kj, q0) for hd in range(HEADS)]
        for hd in range(HEADS):
            online_update(hd, jnp.where(causal, ss[hd], MASKED), vt_ref[0, hd, kj], q0)

    def kv_body(step, carry):
        ss = [scores(hd, step * sub_blocks, 0) for hd in range(HEADS)]
        for d in range(sub_blocks):
            kj = step * sub_blocks + d
            ss_next = []
            for hd in range(HEADS):
                if d + 1 < sub_blocks:
                    ss_next.append(scores(hd, kj + 1, 0))
                online_update(hd, ss[hd], vt_ref[0, hd, kj], 0)
            ss = ss_next
        return carry

    lax.fori_loop(0, qi, kv_body, 0)

    ys = []
    for hd in range(HEADS):
        gate = qg_s[:, aw + hd * HEAD_DIM:aw + (hd + 1) * HEAD_DIM]
        o = (ot_s[hd, 0:HEAD_DIM, :] / ot_s[hd, HEAD_DIM:HEAD_DIM + 1, :]).T
        ys.append((o * (gate * _sigmoid(gate))).astype(BF16))
    y = jnp.concatenate(ys, axis=1)
    out_ref[0] = x + jnp.dot(y, wout_ref[0], preferred_element_type=F32)


def _b_layer(x, qa, k, ka, vt, km, kam, vtm, p, *, bq, bk):
    bsz, t, d = x.shape
    assert t % bq == 0 and bq % bk == 0
    nkb = t // bk
    aw = ATTN_WIDTH
    const2 = lambda b, q: (0, 0)
    layer3 = lambda b, q: (p["layer"], 0, 0)
    resident = dict(pipeline_mode=pl.Buffered(1))
    return pl.pallas_call(
        functools.partial(_b_layer_kernel, bq=bq, bk=bk),
        grid=(bsz, t // bq),
        in_specs=[
            pl.BlockSpec((1, bq, d), lambda b, q: (b, q, 0)),
            pl.BlockSpec((1, bq, V7X_LANES), lambda b, q: (b, q, 0)),
            pl.BlockSpec((1, HEADS, t, HEAD_DIM), lambda b, q: (b, 0, 0, 0)),
            pl.BlockSpec((1, t, V7X_LANES), lambda b, q: (b, 0, 0)),
            pl.BlockSpec((1, HEADS, nkb, V_ROWS, bk), lambda b, q: (b, 0, 0, 0, 0)),
            pl.BlockSpec((1, HEADS, V7X_LANES, HEAD_DIM), lambda b, q: (0, 0, 0, 0)),
            pl.BlockSpec((1, V7X_LANES, V7X_LANES), lambda b, q: (0, 0, 0)),
            pl.BlockSpec((1, HEADS, 1, V_ROWS, V7X_LANES), lambda b, q: (0, 0, 0, 0, 0)),
            pl.BlockSpec((1, d), const2),
            pl.BlockSpec((1, d, 2 * aw), layer3, **resident),
            pl.BlockSpec((1, HEAD_DIM), const2),
            pl.BlockSpec((1, aw, d), layer3, **resident),
        ],
        out_specs=pl.BlockSpec((1, bq, d), lambda b, q: (b, q, 0)),
        out_shape=jax.ShapeDtypeStruct((bsz, t, d), F32),
        scratch_shapes=[
            pltpu.VMEM((bq, 2 * aw), F32),
            pltpu.VMEM((HEADS, 2 * HEAD_DIM, bq), BF16),
            pltpu.VMEM((HEADS, V_ROWS, bq), F32),
            pltpu.VMEM((HEADS, 1, bq), F32),
        ],
        compiler_params=pltpu.CompilerParams(
            dimension_semantics=("arbitrary", "arbitrary"), vmem_limit_bytes=VMEM_LIMIT),
        name="fox_attention_layer",
    )(x, qa, k, ka, vt, km, kam, vtm, p["gn"], p["win"], p["gq"], p["wout"])


def _a_params(l, a_norm, a_w_in, a_conv_w, a_conv_b, a_w_r, a_b_r, a_w_i, a_b_i, a_lambda, a_w_out):
    wri = jnp.concatenate([a_w_r[l], a_w_i[l]], axis=-1).astype(F32)
    bri = jnp.concatenate([a_b_r[l].reshape(LRU_BLOCKS, 1, LRU_BLOCK),
                           a_b_i[l].reshape(LRU_BLOCKS, 1, LRU_BLOCK)], axis=-1).astype(F32)
    return dict(layer=l, gn=a_norm[l].reshape(1, -1).astype(F32), win=a_w_in.astype(F32),
                cw=a_conv_w[l].astype(F32), cb=a_conv_b[l].reshape(1, -1).astype(F32),
                wri=wri, bri=bri, lam=a_lambda[l].reshape(1, -1).astype(F32),
                wout=a_w_out.astype(F32))


def _kv_params(kv_norm, w_kv, b_f, k_norm):
    aw = ATTN_WIDTH
    wf = jnp.repeat(w_kv[:, 2 * aw:2 * aw + HEADS].astype(F32), AUG_LANES, axis=1)
    bfp = jnp.repeat(b_f.astype(F32), AUG_LANES).reshape(1, V7X_LANES)
    return dict(gn=kv_norm.reshape(1, -1).astype(F32), wkv=w_kv.astype(F32).T, wf=wf.astype(BF16),
                bf=bfp, gk=k_norm.reshape(1, -1).astype(F32))


def kernel(x, meta_tokens, a_norm, a_w_in, a_conv_w, a_conv_b, a_w_r, a_b_r, a_w_i, a_b_i, a_lambda,
           a_w_out, kv_norm, w_kv, b_f, k_norm, b_norm, b_w_in, q_norm, b_w_out):
    n_a = a_norm.shape[0]
    n_b = b_norm.shape[0]
    h = x.astype(F32)
    hm = meta_tokens[None].astype(F32)
    zero_state = jnp.zeros((16, LRU_WIDTH), F32)

    for l in range(n_a):
        p = _a_params(l, a_norm, a_w_in, a_conv_w, a_conv_b, a_w_r, a_b_r, a_w_i, a_b_i, a_lambda,
                      a_w_out)
        hm, st = _a_layer(hm, zero_state, p, chunk=N_META)
        h = _a_layer_perm(h, st[0], p, chunk=A_CHUNK)

    pkv = _kv_params(kv_norm, w_kv, b_f, k_norm)
    hm_pad = jnp.pad(hm, ((0, 0), (0, V7X_LANES - N_META), (0, 0)))
    c_zero = jnp.zeros((V7X_SUBLANES, V7X_LANES), F32)
    km, kam, vtm, _, c_meta = _kv_proj(hm_pad, c_zero, pkv, chunk=V7X_LANES, kblock=V7X_LANES,
                                       n_valid=N_META)
    k, ka, vt, qa, _, b_win, b_wout = _kv_proj(h, c_meta[0], pkv, chunk=KV_CHUNK, kblock=ATT_BLOCK,
                                               n_valid=h.shape[1],
                                               extra=(b_w_in.astype(F32), b_w_out.astype(F32)))

    for j in range(n_b):
        p = dict(layer=j, gn=b_norm[j].reshape(1, -1).astype(F32), win=b_win,
                 gq=q_norm[j].reshape(1, -1).astype(F32), wout=b_wout)
        h = _b_layer(h, qa, k, ka, vt, km, kam, vtm, p, bq=ATT_QUERIES, bk=ATT_BLOCK)
    return h
```
